```python
import jax, jax.numpy as jnp
from jax import lax
import numpy as np

D_MODEL = 1024
BATCH = 8
SEQ = 2048
DEPTH = 2

CTX_LEN = 256
GRID_W = 64
N_MIXERS = 2
HGRN_HEADS = 8
HGRN_HEAD_DIM = D_MODEL // HGRN_HEADS
HGRN_WIDTH = HGRN_HEADS * HGRN_HEAD_DIM
CHUNK = 64
POOL_WINDOWS = (2, 4, 8, 16)
POOL_WIDTH = D_MODEL
POOL_GROUPS = len(POOL_WINDOWS)
POOL_GROUP_DIM = POOL_WIDTH // POOL_GROUPS
N_HGRN_LAYERS = (DEPTH + 1) // 2
N_POOL_LAYERS = DEPTH // 2
EPS = 1e-6

kernel_name = 'hybrid_hgrn2_pool_prefix_dit'


def rmsnorm(x, w):
    xf = x.astype(jnp.float32)
    y = xf * lax.rsqrt(jnp.mean(xf * xf, axis=-1, keepdims=True) + EPS)
    return (y * w.astype(jnp.float32)).astype(x.dtype)


def _adaln(cond, w, b):
    mod = jax.nn.silu(cond) @ w + b
    return jnp.split(mod[:, None, :], 3, axis=-1)


def lower_bounds(logits):
    lg = jnp.concatenate([logits, jnp.zeros_like(logits[:1])], axis=0).astype(jnp.float32)
    p = jax.nn.softmax(lg, axis=0)
    return jnp.cumsum(p, axis=0)[: logits.shape[0]]


def _heads(a):
    b, t, _ = a.shape
    return a.reshape(b, t, HGRN_HEADS, HGRN_HEAD_DIM).transpose(0, 2, 1, 3)


def _chunk(a):
    b, h, t, d = a.shape
    return a.reshape(b, h, t // CHUNK, CHUNK, d)


def _inter_chunk_states(k_c, v_c, b_c, s0):
    b_last = b_c[:, :, :, -1]
    k_end = k_c * jnp.exp(b_last[:, :, :, None, :] - b_c)
    u = jnp.einsum('bhnck,bhncv->bhnkv', k_end, v_c)
    decay = jnp.exp(b_last)

    def step(s, inp):
        d_n, u_n = inp
        return d_n[..., None] * s + u_n, s

    s_final, s_prev = lax.scan(step, s0, (jnp.moveaxis(decay, 2, 0), jnp.moveaxis(u, 2, 0)))
    return jnp.moveaxis(s_prev, 0, 2), s_final


def gla_forward(q, k, v, log_f, s0):
    b, h, t, _ = q.shape
    q_c, k_c, v_c = _chunk(q), _chunk(k), _chunk(v)
    b_c = jnp.cumsum(_chunk(log_f), axis=3)
    s_prev, _ = _inter_chunk_states(k_c, v_c, b_c, s0)
    o_inter = jnp.einsum('bhnck,bhnkv->bhncv', q_c * jnp.exp(b_c), s_prev)
    b_mid = b_c[:, :, :, CHUNK // 2 - 1:CHUNK // 2]
    a = jnp.einsum('bhnck,bhnsk->bhncs', q_c * jnp.exp(b_c - b_mid), k_c * jnp.exp(b_mid - b_c))
    a = jnp.where(jnp.tril(jnp.ones((CHUNK, CHUNK), dtype=bool)), a, 0.0)
    o_intra = jnp.einsum('bhncs,bhnsv->bhncv', a, v_c)
    return (o_inter + o_intra).reshape(b, h, t, -1)


def gla_final_state(k, v, log_f, s0):
    k_c, v_c = _chunk(k), _chunk(v)
    b_c = jnp.cumsum(_chunk(log_f), axis=3)
    _, s_final = _inter_chunk_states(k_c, v_c, b_c, s0)
    return s_final


def _rev(a):
    return jnp.flip(a, axis=2)


def _hgrn2_gates(h, w_in, lb):
    f_f, f_b, i = jnp.split((h @ w_in[:, : 3 * HGRN_WIDTH]).astype(jnp.float32), 3, axis=-1)
    out = []
    for d, f_pre in enumerate((f_f, f_b)):
        f = lb[d] + (1.0 - lb[d]) * jax.nn.sigmoid(f_pre)
        out.append((_heads(1.0 - f), _heads(jnp.log(f))))
    return out[0], out[1], _heads(i)


def hgrn2_final_states(h, w_in, lb):
    (k_f, lf_f), (k_b, lf_b), v = _hgrn2_gates(h, w_in, lb)
    zeros = jnp.zeros(v.shape[:2] + (HGRN_HEAD_DIM, HGRN_HEAD_DIM), jnp.float32)
    s_f = gla_final_state(k_f, v, lf_f, zeros)
    s_b = gla_final_state(_rev(k_b), _rev(v), _rev(lf_b), zeros)
    return s_f, s_b


def hgrn2_mixer(h, w_in, lb, gnorm_w, w_out, s0_f, s0_b):
    bsz, t, _ = h.shape
    (k_f, lf_f), (k_b, lf_b), v = _hgrn2_gates(h, w_in, lb)
    q, z = jnp.split(h @ w_in[:, 3 * HGRN_WIDTH:], 2, axis=-1)
    q = _heads(jax.nn.silu(q).astype(jnp.float32)) * (HGRN_HEAD_DIM ** -0.5)
    o_f = gla_forward(q, k_f, v, lf_f, s0_f)
    o_b = _rev(gla_forward(_rev(q), _rev(k_b), _rev(v), _rev(lf_b), s0_b))
    o = (o_f + o_b).transpose(0, 2, 1, 3)
    o = rmsnorm(o, gnorm_w.reshape(HGRN_HEADS, HGRN_HEAD_DIM))
    o = o.reshape(bsz, t, HGRN_WIDTH).astype(h.dtype) * jax.nn.silu(z)
    return o @ w_out


def _centred_mean(u, w):
    length = u.shape[-2]
    csum = jnp.concatenate([jnp.zeros_like(u[..., :1, :]), jnp.cumsum(u, axis=-2)], axis=-2)
    t = jnp.arange(length)
    lo = jnp.clip(t - w // 2, 0, length)
    hi = jnp.clip(t - w // 2 + w, 0, length)
    s = jnp.take(csum, hi, axis=-2) - jnp.take(csum, lo, axis=-2)
    return s / (hi - lo).astype(u.dtype)[:, None]


def pool_mixer(h, rows, row_len, w_in, w_grp, scale, w_out):
    bsz, t, _ = h.shape
    u, z = jnp.split(h @ w_in, 2, axis=-1)
    ug = u.astype(jnp.float32).reshape(bsz, rows, row_len, POOL_GROUPS, POOL_GROUP_DIM)
    pooled = jnp.stack([_centred_mean(ug[..., g, :], w) for g, w in enumerate(POOL_WINDOWS)], axis=-2) - ug
    y = jnp.einsum('brwgi,gio->brwgo', pooled, w_grp).reshape(bsz, t, POOL_WIDTH) * scale
    return (y.astype(h.dtype) * jax.nn.silu(z)) @ w_out


def setup_inputs(seed: int = 0) -> dict:
    key = jax.random.key(seed)
    ks = jax.random.split(key, 18)
    d, e, g = D_MODEL, HGRN_WIDTH, POOL_GROUP_DIM
    n = jax.random.normal
    return {
        'x': n(ks[0], (BATCH, SEQ, d), jnp.float32),
        'c': n(ks[1], (BATCH, d), jnp.float32),
        'ctx': n(ks[2], (BATCH, CTX_LEN, d), jnp.float32),
        'c_ctx': n(ks[3], (d,), jnp.float32),
        'ada_w': n(ks[4], (DEPTH, d, 3 * d), jnp.float32) * (0.5 * d ** -0.5),
        'ada_b': n(ks[5], (DEPTH, 3 * d), jnp.float32) * 0.02,
        'norm_w': 1.0 + 0.05 * n(ks[6], (DEPTH, d), jnp.float32),
        'hgrn_w_in': n(ks[7], (N_HGRN_LAYERS, d, 5 * e), jnp.float32) * d ** -0.5,
        'hgrn_lb_logits': 0.1 * n(ks[8], (N_HGRN_LAYERS, 2, e), jnp.float32),
        'hgrn_gnorm_w': 1.0 + 0.05 * n(ks[9], (N_HGRN_LAYERS, e), jnp.float32),
        'hgrn_w_out': n(ks[10], (N_HGRN_LAYERS, e, d), jnp.float32) * e ** -0.5,
        'pool_w_in': n(ks[11], (N_POOL_LAYERS, d, 2 * POOL_WIDTH), jnp.float32) * d ** -0.5,
        'pool_w_grp': n(ks[12], (N_POOL_LAYERS, POOL_GROUPS, g, g), jnp.float32) * g ** -0.5,
        'pool_scale': 1.0 + 0.05 * n(ks[13], (N_POOL_LAYERS, POOL_WIDTH), jnp.float32),
        'pool_w_out': n(ks[14], (N_POOL_LAYERS, POOL_WIDTH, d), jnp.float32) * POOL_WIDTH ** -0.5,
        'final_norm_w': 1.0 + 0.05 * n(ks[15], (d,), jnp.float32),
    }


def reference(x, c, ctx, c_ctx, ada_w, ada_b, norm_w, hgrn_w_in, hgrn_lb_logits, hgrn_gnorm_w,
              hgrn_w_out, pool_w_in, pool_w_grp, pool_scale, pool_w_out, final_norm_w):
    rows = x.shape[1] // GRID_W
    lbs = lower_bounds(hgrn_lb_logits)
    h_ctx = ctx
    for i in range(DEPTH):
        is_hgrn = (i % N_MIXERS == 0)
        j = i // N_MIXERS
        ctx_update = any(l % N_MIXERS == 0 for l in range(i + 1, DEPTH))
        shift, scale, gate = _adaln(c, ada_w[i], ada_b[i])
        hx = rmsnorm(x, norm_w[i]) * (1.0 + scale) + shift
        if is_hgrn or ctx_update:
            c_shift, c_scale, c_gate = _adaln(c_ctx[None, :], ada_w[i], ada_b[i])
            hc = rmsnorm(h_ctx, norm_w[i]) * (1.0 + c_scale) + c_shift
        if is_hgrn:
            s_f, s_b = hgrn2_final_states(hc, hgrn_w_in[j], lbs[j])
            x = x + gate * hgrn2_mixer(hx, hgrn_w_in[j], lbs[j], hgrn_gnorm_w[j], hgrn_w_out[j], s_f, s_b)
            if ctx_update:
                zeros = jnp.zeros_like(s_f)
                h_ctx = h_ctx + c_gate * hgrn2_mixer(hc, hgrn_w_in[j], lbs[j], hgrn_gnorm_w[j], hgrn_w_out[j], zeros, zeros)
        else:
            x = x + gate * pool_mixer(hx, rows, GRID_W, pool_w_in[j], pool_w_grp[j], pool_scale[j], pool_w_out[j])
            if ctx_update:
                h_ctx = h_ctx + c_gate * pool_mixer(hc, 1, hc.shape[1], pool_w_in[j], pool_w_grp[j], pool_scale[j], pool_w_out[j])
    return rmsnorm(x, final_norm_w)
```

```python
import functools

import jax
import jax.numpy as jnp
from jax import lax
from jax.experimental import pallas as pl
from jax.experimental.pallas import tpu as pltpu

HEADS = 8
CHUNK = 64
GRID_W = 64
POOL_WINDOWS = (2, 4, 8, 16)
EPS = 1e-6
TOKENS_PER_TILE = 256
ADA_COLS_PER_STEP = 512
COND_ROWS = 16
VMEM_LIMIT_BYTES = 56 * 1024 * 1024

F32 = jnp.float32
BF16 = jnp.bfloat16


def _silu(a):
    return a * jax.nn.sigmoid(a)


def _rmsnorm(xf, w):
    y = xf * lax.rsqrt(jnp.mean(xf * xf, axis=-1, keepdims=True) + EPS)
    return y * w


def _split_bf16(a):
    hi = a.astype(BF16)
    lo = (a - hi.astype(F32)).astype(BF16)
    return hi, lo


def _dot(a, b):
    return jnp.dot(a, b, preferred_element_type=F32)


def _dot_nt(a, b):
    return lax.dot_general(a, b, (((1,), (1,)), ((), ())), preferred_element_type=F32)


def _dot_tn(a, b):
    return lax.dot_general(a, b, (((0,), (0,)), ((), ())), preferred_element_type=F32)


def _lower_bound(logits, layer):
    rows = [logits[n] for n in range(logits.shape[0])]
    m = jnp.zeros_like(rows[0])
    for r in rows:
        m = jnp.maximum(m, r)
    exps = [jnp.exp(r - m) for r in rows]
    denom = jnp.exp(-m)
    for e in exps:
        denom = denom + e
    acc = exps[0]
    for e in exps[1:layer + 1]:
        acc = acc + e
    return acc / denom


def _gate_and_decay(f_pre, lb, cum_ref):
    f = lb + (1.0 - lb) * jax.nn.sigmoid(f_pre)
    hi, lo = _split_bf16(jnp.log(f))
    cum = cum_ref[...]
    return 1.0 - f, _dot(cum, hi) + _dot(cum, lo)


def _scan_chunks(n_chunks, reverse, k_ref, b_ref, v_ref, q_ref, o_ref, st_ref):
    mid = CHUNK // 2 if reverse else CHUNK // 2 - 1
    last = 0 if reverse else CHUNK - 1
    dh = st_ref.shape[-1]
    row = lax.broadcasted_iota(jnp.int32, (CHUNK, CHUNK), 0)
    col = lax.broadcasted_iota(jnp.int32, (CHUNK, CHUNK), 1)
    keep = (col >= row) if reverse else (col <= row)

    def body(i, carry):
        ci = (n_chunks - 1 - i) if reverse else i
        rows = pl.ds(pl.multiple_of(ci * CHUNK, CHUNK), CHUNK)
        for h in range(HEADS):
            lanes = slice(h * dh, (h + 1) * dh)
            b = b_ref[rows, lanes]
            k = k_ref[rows, lanes]
            v = v_ref[rows, lanes].astype(BF16)
            b_mid = b[mid:mid + 1]
            b_last = b[last:last + 1]
            k_dec = k * jnp.exp(b_mid - b)
            st = st_ref[h]
            if q_ref is not None:
                q_dec = q_ref[rows, lanes] * jnp.exp(b - b_mid)
                a = _dot_nt(q_dec.astype(BF16), k_dec.astype(BF16))
                a = jnp.where(keep, a, 0.0)
                o = _dot(a.astype(BF16), v)
                q_in = (q_dec * jnp.exp(b_mid)).astype(BF16)
                o = o + _dot_nt(q_in, st.astype(BF16))
                o_ref[rows, lanes] = o
            k_end = (k_dec * jnp.exp(b_last - b_mid)).astype(BF16)
            st_ref[h] = st * jnp.exp(b_last) + _dot_tn(v, k_end)
        return carry

    lax.fori_loop(0, n_chunks, body, 0)


def _ada_body(cond_ref, w_ref, b_ref, o_ref):
    o_ref[0] = _dot(_silu(cond_ref[...]), w_ref[0]) + b_ref[0]


def _ada_mods(cond, ada_w, ada_b):
    depth, d, n = ada_w.shape
    nb = ADA_COLS_PER_STEP
    return pl.pallas_call(
        _ada_body,
        grid=(depth, n // nb),
        in_specs=[
            pl.BlockSpec((COND_ROWS, d), lambda l, j: (0, 0)),
            pl.BlockSpec((1, d, nb), lambda l, j: (l, 0, j)),
            pl.BlockSpec((1, 1, nb), lambda l, j: (l, 0, j)),
        ],
        out_specs=pl.BlockSpec((1, COND_ROWS, nb), lambda l, j: (l, 0, j)),
        out_shape=jax.ShapeDtypeStruct((depth, COND_ROWS, n), F32),
        name="ada_mods",
    )(cond, ada_w, ada_b.reshape(depth, 1, n))


def _ctx_body(ctx_ref, shift_ref, scale_ref, nw_ref, w_ref, lg_ref, cumf_ref, cumb_ref,
              sf_ref, sb_ref, proj_ref, k_ref, b_ref):
    d = ctx_ref.shape[-1]
    n_chunks = ctx_ref.shape[1] // CHUNK
    hc = _rmsnorm(ctx_ref[0], nw_ref[...]) * (1.0 + scale_ref[...]) + shift_ref[...]
    proj_ref[...] = _dot(hc.astype(BF16), w_ref[...])
    v_ref = proj_ref.at[:, 2 * d:3 * d]
    for direction, (cum_ref, st_ref) in enumerate(((cumf_ref, sf_ref), (cumb_ref, sb_ref))):
        lb = _lower_bound(lg_ref[:, direction], 0)
        k, b = _gate_and_decay(proj_ref[:, direction * d:(direction + 1) * d], lb, cum_ref)
        k_ref[...] = k
        b_ref[...] = b
        st = st_ref.at[0]
        st[...] = jnp.zeros(st.shape, F32)
        _scan_chunks(n_chunks, direction == 1, k_ref, b_ref, v_ref, None, None, st)


def _ctx_states(ctx, shift, scale, norm_w, w_ctx, lb_logits, cum_f, cum_b):
    bsz, tc, d = ctx.shape
    dh = d // HEADS
    const2 = lambda b: (0, 0)
    state_spec = pl.BlockSpec((1, HEADS, dh, dh), lambda b: (b, 0, 0, 0))
    state_shape = jax.ShapeDtypeStruct((bsz, HEADS, dh, dh), F32)
    return pl.pallas_call(
        _ctx_body,
        grid=(bsz,),
        in_specs=[
            pl.BlockSpec((1, tc, d), lambda b: (b, 0, 0)),
            pl.BlockSpec((1, d), const2),
            pl.BlockSpec((1, d), const2),
            pl.BlockSpec((1, d), const2),
            pl.BlockSpec(w_ctx.shape, const2),
            pl.BlockSpec(lb_logits.shape, lambda b: (0, 0, 0, 0)),
            pl.BlockSpec(cum_f.shape, const2),
            pl.BlockSpec(cum_b.shape, const2),
        ],
        out_specs=[state_spec, state_spec],
        out_shape=[state_shape, state_shape],
        scratch_shapes=[
            pltpu.VMEM((tc, 3 * d), F32),
            pltpu.VMEM((tc, d), F32),
            pltpu.VMEM((tc, d), F32),
        ],
        compiler_params=pltpu.CompilerParams(vmem_limit_bytes=VMEM_LIMIT_BYTES),
        name="ctx_states",
    )(ctx, shift, scale, norm_w, w_ctx, lb_logits, cum_f, cum_b)


def _project(x_ref, shift_ref, scale_ref, nw_ref, w_ref, proj_ref):
    hx = _rmsnorm(x_ref[0], nw_ref[...]) * (1.0 + scale_ref[0]) + shift_ref[0]
    proj_ref[...] = _dot(hx.astype(BF16), w_ref[...])


def _sweep(direction, proj_ref, lg_ref, cum_ref, s0_ref, k_ref, b_ref, o_ref, st_ref):
    d = k_ref.shape[-1]
    dh = d // HEADS

    @pl.when(pl.program_id(1) == 0)
    def _():
        st_ref[...] = s0_ref[0]

    lb = _lower_bound(lg_ref[:, direction], 0)
    k, b = _gate_and_decay(proj_ref[:, 0:d], lb, cum_ref)
    k_ref[...] = k
    b_ref[...] = b
    proj_ref[:, 2 * d:3 * d] = _silu(proj_ref[:, 2 * d:3 * d]) * (dh ** -0.5)
    _scan_chunks(k_ref.shape[0] // CHUNK, direction == 1, k_ref, b_ref,
                 proj_ref.at[:, d:2 * d], proj_ref.at[:, 2 * d:3 * d], o_ref, st_ref)


def _fwd_body(x_ref, shift_ref, scale_ref, nw_ref, w_ref, lg_ref, cum_ref, s0_ref,
              o_ref, proj_ref, k_ref, b_ref, st_ref):
    _project(x_ref, shift_ref, scale_ref, nw_ref, w_ref, proj_ref)
    _sweep(0, proj_ref, lg_ref, cum_ref, s0_ref, k_ref, b_ref, o_ref.at[0], st_ref)


def _bwd_body(x_ref, shift_ref, scale_ref, gate_ref, nw_ref, w_ref, lg_ref, cum_ref, s0_ref,
              of_ref, gw_ref, wo_ref, x1_ref, proj_ref, k_ref, b_ref, ob_ref, st_ref):
    d = k_ref.shape[-1]
    dh = d // HEADS
    _project(x_ref, shift_ref, scale_ref, nw_ref, w_ref, proj_ref)
    _sweep(1, proj_ref, lg_ref, cum_ref, s0_ref, k_ref, b_ref, ob_ref, st_ref)
    for h in range(HEADS):
        lanes = slice(h * dh, (h + 1) * dh)
        o = of_ref[0, :, lanes] + ob_ref[:, lanes]
        z = proj_ref[:, 3 * d + h * dh:3 * d + (h + 1) * dh]
        ob_ref[:, lanes] = _rmsnorm(o, gw_ref[:, lanes]) * _silu(z)
    mix = _dot(ob_ref[...].astype(BF16), wo_ref[...])
    x1_ref[0] = x_ref[0] + gate_ref[0] * mix


def _hgrn_layer(x, shift, scale, gate, norm_w, w_fwd, w_bwd, lb_logits, cum_f, cum_b,
                s0_f, s0_b, gnorm_w, w_out):
    bsz, t, d = x.shape
    dh = d // HEADS
    tq = TOKENS_PER_TILE
    nt = t // tq
    params = pltpu.CompilerParams(vmem_limit_bytes=VMEM_LIMIT_BYTES)

    def common_specs(tile):
        row = pl.BlockSpec((1, 1, d), lambda b, i: (b, 0, 0))
        return tile, row

    def tail_specs(w):
        return [
            pl.BlockSpec((1, d), lambda b, i: (0, 0)),
            pl.BlockSpec(w.shape, lambda b, i: (0, 0)),
            pl.BlockSpec(lb_logits.shape, lambda b, i: (0, 0, 0, 0)),
            pl.BlockSpec((tq, tq), lambda b, i: (0, 0)),
            pl.BlockSpec((1, HEADS, dh, dh), lambda b, i: (b, 0, 0, 0)),
        ]

    tile_f = pl.BlockSpec((1, tq, d), lambda b, i: (b, i, 0))
    tile_f, row = common_specs(tile_f)
    o_f = pl.pallas_call(
        _fwd_body,
        grid=(bsz, nt),
        in_specs=[tile_f, row, row] + tail_specs(w_fwd),
        out_specs=tile_f,
        out_shape=jax.ShapeDtypeStruct((bsz, t, d), F32),
        scratch_shapes=[
            pltpu.VMEM((tq, 3 * d), F32),
            pltpu.VMEM((tq, d), F32),
            pltpu.VMEM((tq, d), F32),
            pltpu.VMEM((HEADS, dh, dh), F32),
        ],
        compiler_params=params,
        name="hgrn_fwd_sweep",
    )(x, shift, scale, norm_w, w_fwd, lb_logits, cum_f, s0_f)

    tile_b = pl.BlockSpec((1, tq, d), lambda b, i: (b, nt - 1 - i, 0))
    return pl.pallas_call(
        _bwd_body,
        grid=(bsz, nt),
        in_specs=[tile_b, row, row, row] + tail_specs(w_bwd) + [
            tile_b,
            pl.BlockSpec((1, d), lambda b, i: (0, 0)),
            pl.BlockSpec((d, d), lambda b, i: (0, 0)),
        ],
        out_specs=tile_b,
        out_shape=jax.ShapeDtypeStruct((bsz, t, d), F32),
        scratch_shapes=[
            pltpu.VMEM((tq, 4 * d), F32),
            pltpu.VMEM((tq, d), F32),
            pltpu.VMEM((tq, d), F32),
            pltpu.VMEM((tq, d), F32),
            pltpu.VMEM((HEADS, dh, dh), F32),
        ],
        compiler_params=params,
        name="hgrn_bwd_sweep",
    )(x, shift, scale, gate, norm_w, w_bwd, lb_logits, cum_b, s0_b, o_f, gnorm_w, w_out)


def _pool_body(x_ref, shift_ref, scale_ref, gate_ref, nw_ref, wi_ref, band_ref, wg_ref,
               ps_ref, wo_ref, fw_ref, out_ref, act_ref):
    d = x_ref.shape[-1]
    tq = x_ref.shape[1]
    gd = d // len(POOL_WINDOWS)
    x = x_ref[0]
    hx = _rmsnorm(x, nw_ref[...]) * (1.0 + scale_ref[0]) + shift_ref[0]
    uz = _dot(hx.astype(BF16), wi_ref[...])
    pos = lax.broadcasted_iota(jnp.int32, (tq, gd), 0) % GRID_W
    for g, w in enumerate(POOL_WINDOWS):
        lanes = slice(g * gd, (g + 1) * gd)
        u = uz[:, lanes]
        lo = jnp.maximum(pos - w // 2, 0)
        hi = jnp.minimum(pos - w // 2 + w, GRID_W)
        u_hi, u_lo = _split_bf16(u)
        band = band_ref[g]
        window_sum = _dot(band, u_hi) + _dot(band, u_lo)
        pooled = window_sum / (hi - lo).astype(F32) - u
        y = _dot(pooled.astype(BF16), wg_ref[g]) * ps_ref[:, lanes]
        act_ref[:, lanes] = (y * _silu(uz[:, d + g * gd:d + (g + 1) * gd])).astype(BF16)
    mix = _dot(act_ref[...], wo_ref[...])
    out_ref[0] = _rmsnorm(x + gate_ref[0] * mix, fw_ref[...])


def _pool_layer(x, shift, scale, gate, norm_w, w_in, band, w_grp, pool_scale, w_out, final_w):
    bsz, t, d = x.shape
    tq = TOKENS_PER_TILE
    tile = pl.BlockSpec((1, tq, d), lambda b, i: (b, i, 0))
    row = pl.BlockSpec((1, 1, d), lambda b, i: (b, 0, 0))
    vec = pl.BlockSpec((1, d), lambda b, i: (0, 0))
    return pl.pallas_call(
        _pool_body,
        grid=(bsz, t // tq),
        in_specs=[
            tile, row, row, row, vec,
            pl.BlockSpec(w_in.shape, lambda b, i: (0, 0)),
            pl.BlockSpec(band.shape, lambda b, i: (0, 0, 0)),
            pl.BlockSpec(w_grp.shape, lambda b, i: (0, 0, 0)),
            vec,
            pl.BlockSpec(w_out.shape, lambda b, i: (0, 0)),
            vec,
        ],
        out_specs=tile,
        out_shape=jax.ShapeDtypeStruct((bsz, t, d), F32),
        scratch_shapes=[pltpu.VMEM((tq, d), BF16)],
        compiler_params=pltpu.CompilerParams(vmem_limit_bytes=VMEM_LIMIT_BYTES),
        name="pool_layer",
    )(x, shift, scale, gate, norm_w, w_in, band, w_grp, pool_scale, w_out, final_w)


def _cumulative_matrices(n):
    r = jnp.arange(n)[:, None]
    c = jnp.arange(n)[None, :]
    same = (r // CHUNK) == (c // CHUNK)
    return (same & (c <= r)).astype(BF16), (same & (c >= r)).astype(BF16)


def _band_matrices(n):
    r = jnp.arange(n)[:, None]
    c = jnp.arange(n)[None, :]
    same = (r // GRID_W) == (c // GRID_W)
    return jnp.stack([(same & (c >= r - w // 2) & (c < r - w // 2 + w)).astype(BF16)
                      for w in POOL_WINDOWS])


def kernel(x, c, ctx, c_ctx, ada_w, ada_b, norm_w, hgrn_w_in, hgrn_lb_logits, hgrn_gnorm_w,
           hgrn_w_out, pool_w_in, pool_w_grp, pool_scale, pool_w_out, final_norm_w):
    bsz, t, d = x.shape
    depth = ada_w.shape[0]
    assert depth == 2 and hgrn_w_in.shape[0] == 1 and pool_w_in.shape[0] == 1
    assert t % TOKENS_PER_TILE == 0 and ctx.shape[1] % CHUNK == 0 and bsz < COND_ROWS
    assert TOKENS_PER_TILE % GRID_W == 0 and GRID_W == CHUNK

    cond = jnp.zeros((COND_ROWS, d), F32).at[:bsz].set(c).at[bsz].set(c_ctx)
    mods = _ada_mods(cond, ada_w, ada_b)
    shift, scale, gate = (mods[:, :, i * d:(i + 1) * d] for i in range(3))
    per_batch = lambda m, l: m[l, :bsz].reshape(bsz, 1, d)

    w_in = hgrn_w_in[0].astype(BF16)
    w_ctx = w_in[:, :3 * d]
    w_fwd = jnp.concatenate([w_in[:, :d], w_in[:, 2 * d:4 * d]], axis=1)
    w_bwd = w_in[:, d:]
    lb_logits = hgrn_lb_logits.reshape(hgrn_lb_logits.shape[0], 2, 1, d)
    cum_f, cum_b = _cumulative_matrices(TOKENS_PER_TILE)
    cum_cf, cum_cb = _cumulative_matrices(ctx.shape[1])

    s0_f, s0_b = _ctx_states(ctx, shift[0, bsz:bsz + 1], scale[0, bsz:bsz + 1], norm_w[0:1],
                             w_ctx, lb_logits, cum_cf, cum_cb)
    x1 = _hgrn_layer(x, per_batch(shift, 0), per_batch(scale, 0), per_batch(gate, 0),
                     norm_w[0:1], w_fwd, w_bwd, lb_logits, cum_f, cum_b, s0_f, s0_b,
                     hgrn_gnorm_w[0:1], hgrn_w_out[0].astype(BF16))
    return _pool_layer(x1, per_batch(shift, 1), per_batch(scale, 1), per_batch(gate, 1),
                       norm_w[1:2], pool_w_in[0].astype(BF16), _band_matrices(TOKENS_PER_TILE),
                       pool_w_grp[0].astype(BF16), pool_scale[0:1], pool_w_out[0].astype(BF16),
                       final_norm_w.reshape(1, d))
```

```python
import jax
import jax.numpy as jnp
from jax import lax
from jax.experimental import pallas as pl
from jax.experimental.pallas import tpu as pltpu

HEADS = 8
CHUNK = 64
GRID_W = 64
POOL_WINDOWS = (2, 4, 8, 16)
EPS = 1e-6
TOKENS_PER_TILE = 256
HEADS_PER_GROUP = 2
INTRA_ROWS = 128
ADA_COLS_PER_STEP = 512
COND_ROWS = 16
VMEM_LIMIT_BYTES = 56 * 1024 * 1024

F32 = jnp.float32
BF16 = jnp.bfloat16


def _silu(a):
    return a * jax.nn.sigmoid(a)


def _rmsnorm(xf, w):
    y = xf * lax.rsqrt(jnp.mean(xf * xf, axis=-1, keepdims=True) + EPS)
    return y * w


def _split_bf16(a):
    hi = a.astype(BF16)
    lo = (a - hi.astype(F32)).astype(BF16)
    return hi, lo


def _dot(a, b):
    return jnp.dot(a, b, preferred_element_type=F32)


def _dot_nt(a, b):
    return lax.dot_general(a, b, (((1,), (1,)), ((), ())), preferred_element_type=F32)


def _dot_tn(a, b):
    return lax.dot_general(a, b, (((0,), (0,)), ((), ())), preferred_element_type=F32)


def _lower_bound(logits, layer):
    rows = [logits[n] for n in range(logits.shape[0])]
    m = jnp.zeros_like(rows[0])
    for r in rows:
        m = jnp.maximum(m, r)
    exps = [jnp.exp(r - m) for r in rows]
    denom = jnp.exp(-m)
    for e in exps:
        denom = denom + e
    acc = exps[0]
    for e in exps[1:layer + 1]:
        acc = acc + e
    return acc / denom


def _scan_head(reverse, b, k, v, q, keep, st):
    tq = b.shape[0]
    nc = tq // CHUNK
    per_block = INTRA_ROWS // CHUNK
    mid = CHUNK // 2 if reverse else CHUNK // 2 - 1
    last = 0 if reverse else CHUNK - 1
    k_dec, q_dec, k_end, q_in, decay = [], [], [], [], []
    for n in range(nc):
        rows = slice(n * CHUNK, (n + 1) * CHUNK)
        bn = b[rows]
        b_mid = bn[mid:mid + 1]
        b_last = bn[last:last + 1]
        kd = k[rows] * jnp.exp(b_mid - bn)
        k_dec.append(kd.astype(BF16))
        k_end.append((kd * jnp.exp(b_last - b_mid)).astype(BF16))
        decay.append(jnp.exp(b_last))
        if q is not None:
            qd = q[rows] * jnp.exp(bn - b_mid)
            q_dec.append(qd.astype(BF16))
            q_in.append((qd * jnp.exp(b_mid)).astype(BF16))
    scores = []
    if q is not None:
        for j in range(nc // per_block):
            blk = slice(j * per_block, (j + 1) * per_block)
            scores.append(_dot_nt(jnp.concatenate(q_dec[blk], axis=0),
                                  jnp.concatenate(k_dec[blk], axis=0)))
    update = [_dot_tn(v[n * CHUNK:(n + 1) * CHUNK], k_end[n]) for n in range(nc)]
    yield
    carried = [None] * nc
    for n in (range(nc - 1, -1, -1) if reverse else range(nc)):
        carried[n] = st.astype(BF16)
        st = st * decay[n] + update[n]
    if q is None:
        yield None, st
        return
    scores = [jnp.where(keep, a, 0.0).astype(BF16) for a in scores]
    o_intra = [_dot(a, v[j * INTRA_ROWS:(j + 1) * INTRA_ROWS]) for j, a in enumerate(scores)]
    o_inter = [_dot_nt(q_in[n], carried[n]) for n in range(nc)]
    yield jnp.concatenate(o_intra, axis=0) + jnp.concatenate(o_inter, axis=0), st


def _hgrn_heads(direction, hb, w_ref, col0, lg_ref, cum_ref, st_ref, emit):
    d = hb.shape[-1]
    dh = d // HEADS
    gw = HEADS_PER_GROUP * dh
    n_groups = HEADS // HEADS_PER_GROUP
    reverse = direction == 1
    cum = cum_ref[...].astype(BF16)
    keep = cum_ref[0:INTRA_ROWS, 0:INTRA_ROWS] != 0.0
    lb_all = _lower_bound(lg_ref[:, direction], 0)

    def project(g):
        cols = lambda name: slice(col0[name] + g * gw, col0[name] + (g + 1) * gw)
        return {name: _dot(hb, w_ref[:, cols(name)]) for name in col0}

    def gates(g, raw):
        lb = lb_all[:, g * gw:(g + 1) * gw]
        f = lb + (1.0 - lb) * jax.nn.sigmoid(raw["f"])
        hi, lo = _split_bf16(jnp.log(f))
        b = _dot(cum, hi) + _dot(cum, lo)
        q = _silu(raw["q"]) * (dh ** -0.5) if "q" in raw else None
        return b, 1.0 - f, raw["i"].astype(BF16), q, raw.get("z")

    lookahead = min(2, n_groups)
    raws = [project(g) for g in range(lookahead)]
    ready = [gates(g, raw) for g, raw in enumerate(raws)]
    for g in range(n_groups):
        b, k, v, q, z = ready[g]
        heads = []
        for hh in range(HEADS_PER_GROUP):
            lanes = slice(hh * dh, (hh + 1) * dh)
            stages = _scan_head(reverse, b[:, lanes], k[:, lanes], v[:, lanes],
                                None if q is None else q[:, lanes], keep,
                                st_ref[g * HEADS_PER_GROUP + hh])
            next(stages)
            heads.append(stages)
        raw = project(g + lookahead) if g + lookahead < n_groups else None
        for hh, stages in enumerate(heads):
            h = g * HEADS_PER_GROUP + hh
            o, st = next(stages)
            st_ref[h] = st
            if emit is not None:
                emit(h, o, None if z is None else z[:, hh * dh:(hh + 1) * dh])
        if raw is not None:
            ready.append(gates(g + lookahead, raw))


def _ada_body(cond_ref, w_ref, b_ref, o_ref):
    o_ref[0] = _dot(_silu(cond_ref[...]), w_ref[0]) + b_ref[0]


def _ada_mods(cond, ada_w, ada_b):
    depth, d, n = ada_w.shape
    nb = ADA_COLS_PER_STEP
    return pl.pallas_call(
        _ada_body,
        grid=(depth, n // nb),
        in_specs=[
            pl.BlockSpec((COND_ROWS, d), lambda l, j: (0, 0)),
            pl.BlockSpec((1, d, nb), lambda l, j: (l, 0, j)),
            pl.BlockSpec((1, 1, nb), lambda l, j: (l, 0, j)),
        ],
        out_specs=pl.BlockSpec((1, COND_ROWS, nb), lambda l, j: (l, 0, j)),
        out_shape=jax.ShapeDtypeStruct((depth, COND_ROWS, n), F32),
        name="ada_mods",
    )(cond, ada_w, ada_b.reshape(depth, 1, n))


def _ctx_body(ctx_ref, shift_ref, scale_ref, nw_ref, w_ref, lg_ref, cumf_ref, cumb_ref,
              sf_ref, sb_ref):
    d = ctx_ref.shape[-1]
    hc = _rmsnorm(ctx_ref[0], nw_ref[...]) * (1.0 + scale_ref[...]) + shift_ref[...]
    hb = hc.astype(BF16)
    for direction, (cum_ref, st_ref) in enumerate(((cumf_ref, sf_ref), (cumb_ref, sb_ref))):
        st = st_ref.at[0]
        st[...] = jnp.zeros(st.shape, F32)
        _hgrn_heads(direction, hb, w_ref, {"f": direction * d, "i": 2 * d}, lg_ref, cum_ref,
                    st, None)


def _ctx_states(ctx, shift, scale, norm_w, w_ctx, lb_logits, cum_f, cum_b):
    bsz, tc, d = ctx.shape
    dh = d // HEADS
    const2 = lambda b: (0, 0)
    state_spec = pl.BlockSpec((1, HEADS, dh, dh), lambda b: (b, 0, 0, 0))
    state_shape = jax.ShapeDtypeStruct((bsz, HEADS, dh, dh), F32)
    return pl.pallas_call(
        _ctx_body,
        grid=(bsz,),
        in_specs=[
            pl.BlockSpec((1, tc, d), lambda b: (b, 0, 0)),
            pl.BlockSpec((1, d), const2),
            pl.BlockSpec((1, d), const2),
            pl.BlockSpec((1, d), const2),
            pl.BlockSpec(w_ctx.shape, const2),
            pl.BlockSpec(lb_logits.shape, lambda b: (0, 0, 0, 0)),
            pl.BlockSpec(cum_f.shape, const2),
            pl.BlockSpec(cum_b.shape, const2),
        ],
        out_specs=[state_spec, state_spec],
        out_shape=[state_shape, state_shape],
        compiler_params=pltpu.CompilerParams(vmem_limit_bytes=VMEM_LIMIT_BYTES),
        name="ctx_states",
    )(ctx, shift, scale, norm_w, w_ctx, lb_logits, cum_f, cum_b)


def _normalised_input(x_ref, shift_ref, scale_ref, nw_ref):
    hx = _rmsnorm(x_ref[0], nw_ref[...]) * (1.0 + scale_ref[0]) + shift_ref[0]
    return hx.astype(BF16)


def _load_initial_state(s0_ref, st_ref):
    @pl.when(pl.program_id(1) == 0)
    def _():
        st_ref[...] = s0_ref[0]


def _fwd_body(x_ref, shift_ref, scale_ref, nw_ref, w_ref, lg_ref, cum_ref, s0_ref,
              o_ref, st_ref):
    d = x_ref.shape[-1]
    dh = d // HEADS
    _load_initial_state(s0_ref, st_ref)
    hb = _normalised_input(x_ref, shift_ref, scale_ref, nw_ref)

    def emit(h, o, z):
        o_ref[0, :, h * dh:(h + 1) * dh] = o

    _hgrn_heads(0, hb, w_ref, {"f": 0, "i": d, "q": 2 * d}, lg_ref, cum_ref, st_ref, emit)


def _bwd_body(x_ref, shift_ref, scale_ref, gate_ref, nw_ref, w_ref, lg_ref, cum_ref, s0_ref,
              of_ref, gw_ref, wo_ref, x1_ref, act_ref, st_ref):
    d = x_ref.shape[-1]
    dh = d // HEADS
    _load_initial_state(s0_ref, st_ref)
    hb = _normalised_input(x_ref, shift_ref, scale_ref, nw_ref)

    def emit(h, o, z):
        lanes = slice(h * dh, (h + 1) * dh)
        o = _rmsnorm(of_ref[0, :, lanes] + o, gw_ref[:, lanes])
        act_ref[:, lanes] = (o * _silu(z)).astype(BF16)

    _hgrn_heads(1, hb, w_ref, {"f": 0, "i": d, "q": 2 * d, "z": 3 * d}, lg_ref, cum_ref,
                st_ref, emit)
    x1_ref[0] = x_ref[0] + gate_ref[0] * _dot(act_ref[...], wo_ref[...])


def _hgrn_layer(x, shift, scale, gate, norm_w, w_fwd, w_bwd, lb_logits, cum_f, cum_b,
                s0_f, s0_b, gnorm_w, w_out):
    bsz, t, d = x.shape
    dh = d // HEADS
    tq = TOKENS_PER_TILE
    nt = t // tq
    params = pltpu.CompilerParams(vmem_limit_bytes=VMEM_LIMIT_BYTES)
    row = pl.BlockSpec((1, 1, d), lambda b, i: (b, 0, 0))

    def tail_specs(w):
        return [
            pl.BlockSpec((1, d), lambda b, i: (0, 0)),
            pl.BlockSpec(w.shape, lambda b, i: (0, 0)),
            pl.BlockSpec(lb_logits.shape, lambda b, i: (0, 0, 0, 0)),
            pl.BlockSpec((tq, tq), lambda b, i: (0, 0)),
            pl.BlockSpec((1, HEADS, dh, dh), lambda b, i: (b, 0, 0, 0)),
        ]

    tile_f = pl.BlockSpec((1, tq, d), lambda b, i: (b, i, 0))
    o_f = pl.pallas_call(
        _fwd_body,
        grid=(bsz, nt),
        in_specs=[tile_f, row, row] + tail_specs(w_fwd),
        out_specs=tile_f,
        out_shape=jax.ShapeDtypeStruct((bsz, t, d), F32),
        scratch_shapes=[pltpu.VMEM((HEADS, dh, dh), F32)],
        compiler_params=params,
        name="hgrn_fwd_sweep",
    )(x, shift, scale, norm_w, w_fwd, lb_logits, cum_f, s0_f)

    tile_b = pl.BlockSpec((1, tq, d), lambda b, i: (b, nt - 1 - i, 0))
    return pl.pallas_call(
        _bwd_body,
        grid=(bsz, nt),
        in_specs=[tile_b, row, row, row] + tail_specs(w_bwd) + [
            tile_b,
            pl.BlockSpec((1, d), lambda b, i: (0, 0)),
            pl.BlockSpec((d, d), lambda b, i: (0, 0)),
        ],
        out_specs=tile_b,
        out_shape=jax.ShapeDtypeStruct((bsz, t, d), F32),
        scratch_shapes=[
            pltpu.VMEM((tq, d), BF16),
            pltpu.VMEM((HEADS, dh, dh), F32),
        ],
        compiler_params=params,
        name="hgrn_bwd_sweep",
    )(x, shift, scale, gate, norm_w, w_bwd, lb_logits, cum_b, s0_b, o_f, gnorm_w, w_out)


def _pool_body(x_ref, shift_ref, scale_ref, gate_ref, nw_ref, wi_ref, band_ref, wg_ref,
               ps_ref, wo_ref, fw_ref, out_ref, act_ref):
    d = x_ref.shape[-1]
    tq = x_ref.shape[1]
    gd = d // len(POOL_WINDOWS)
    x = x_ref[0]
    hx = _rmsnorm(x, nw_ref[...]) * (1.0 + scale_ref[0]) + shift_ref[0]
    uz = _dot(hx.astype(BF16), wi_ref[...])
    pos = lax.broadcasted_iota(jnp.int32, (tq, gd), 0) % GRID_W
    for g, w in enumerate(POOL_WINDOWS):
        lanes = slice(g * gd, (g + 1) * gd)
        u = uz[:, lanes]
        lo = jnp.maximum(pos - w // 2, 0)
        hi = jnp.minimum(pos - w // 2 + w, GRID_W)
        u_hi, u_lo = _split_bf16(u)
        band = band_ref[g]
        window_sum = _dot(band, u_hi) + _dot(band, u_lo)
        pooled = window_sum / (hi - lo).astype(F32) - u
        y = _dot(pooled.astype(BF16), wg_ref[g]) * ps_ref[:, lanes]
        act_ref[:, lanes] = (y * _silu(uz[:, d + g * gd:d + (g + 1) * gd])).astype(BF16)
    mix = _dot(act_ref[...], wo_ref[...])
    out_ref[0] = _rmsnorm(x + gate_ref[0] * mix, fw_ref[...])


def _pool_layer(x, shift, scale, gate, norm_w, w_in, band, w_grp, pool_scale, w_out, final_w):
    bsz, t, d = x.shape
    tq = TOKENS_PER_TILE
    tile = pl.BlockSpec((1, tq, d), lambda b, i: (b, i, 0))
    row = pl.BlockSpec((1, 1, d), lambda b, i: (b, 0, 0))
    vec = pl.BlockSpec((1, d), lambda b, i: (0, 0))
    return pl.pallas_call(
        _pool_body,
        grid=(bsz, t // tq),
        in_specs=[
            tile, row, row, row, vec,
            pl.BlockSpec(w_in.shape, lambda b, i: (0, 0)),
            pl.BlockSpec(band.shape, lambda b, i: (0, 0, 0)),
            pl.BlockSpec(w_grp.shape, lambda b, i: (0, 0, 0)),
            vec,
            pl.BlockSpec(w_out.shape, lambda b, i: (0, 0)),
            vec,
        ],
        out_specs=tile,
        out_shape=jax.ShapeDtypeStruct((bsz, t, d), F32),
        scratch_shapes=[pltpu.VMEM((tq, d), BF16)],
        compiler_params=pltpu.CompilerParams(vmem_limit_bytes=VMEM_LIMIT_BYTES),
        name="pool_layer",
    )(x, shift, scale, gate, norm_w, w_in, band, w_grp, pool_scale, w_out, final_w)


def _cumulative_matrices(n):
    r = jnp.arange(n)[:, None]
    c = jnp.arange(n)[None, :]
    same = (r // CHUNK) == (c // CHUNK)
    return (same & (c <= r)).astype(F32), (same & (c >= r)).astype(F32)


def _band_matrices(n):
    r = jnp.arange(n)[:, None]
    c = jnp.arange(n)[None, :]
    same = (r // GRID_W) == (c // GRID_W)
    return jnp.stack([(same & (c >= r - w // 2) & (c < r - w // 2 + w)).astype(BF16)
                      for w in POOL_WINDOWS])


def kernel(x, c, ctx, c_ctx, ada_w, ada_b, norm_w, hgrn_w_in, hgrn_lb_logits, hgrn_gnorm_w,
           hgrn_w_out, pool_w_in, pool_w_grp, pool_scale, pool_w_out, final_norm_w):
    bsz, t, d = x.shape
    depth = ada_w.shape[0]
    assert depth == 2 and hgrn_w_in.shape[0] == 1 and pool_w_in.shape[0] == 1
    assert t % TOKENS_PER_TILE == 0 and ctx.shape[1] % INTRA_ROWS == 0 and bsz < COND_ROWS
    assert TOKENS_PER_TILE % INTRA_ROWS == 0 and INTRA_ROWS % CHUNK == 0 and GRID_W == CHUNK

    cond = jnp.zeros((COND_ROWS, d), F32).at[:bsz].set(c).at[bsz].set(c_ctx)
    mods = _ada_mods(cond, ada_w, ada_b)
    shift, scale, gate = (mods[:, :, i * d:(i + 1) * d] for i in range(3))
    per_batch = lambda m, l: m[l, :bsz].reshape(bsz, 1, d)

    w_in = hgrn_w_in[0].astype(BF16)
    w_ctx = w_in[:, :3 * d]
    w_fwd = jnp.concatenate([w_in[:, :d], w_in[:, 2 * d:4 * d]], axis=1)
    w_bwd = w_in[:, d:]
    lb_logits = hgrn_lb_logits.reshape(hgrn_lb_logits.shape[0], 2, 1, d)
    cum_f, cum_b = _cumulative_matrices(TOKENS_PER_TILE)
    cum_cf, cum_cb = _cumulative_matrices(ctx.shape[1])

    s0_f, s0_b = _ctx_states(ctx, shift[0, bsz:bsz + 1], scale[0, bsz:bsz + 1], norm_w[0:1],
                             w_ctx, lb_logits, cum_cf, cum_cb)
    x1 = _hgrn_layer(x, per_batch(shift, 0), per_batch(scale, 0), per_batch(gate, 0),
                     norm_w[0:1], w_fwd, w_bwd, lb_logits, cum_f, cum_b, s0_f, s0_b,
                     hgrn_gnorm_w[0:1], hgrn_w_out[0].astype(BF16))
    return _pool_layer(x1, per_batch(shift, 1), per_batch(scale, 1), per_batch(gate, 1),
                       norm_w[1:2], pool_w_in[0].astype(BF16), _band_matrices(TOKENS_PER_TILE),
                       pool_w_grp[0].astype(BF16), pool_scale[0:1], pool_w_out[0].astype(BF16),
                       final_norm_w.reshape(1, d))
```

```python
import jax
import jax.numpy as jnp
from jax import lax
from jax.experimental import pallas as pl
from jax.experimental.pallas import tpu as pltpu

HEADS = 8
CHUNK = 64
GRID_W = 64
POOL_WINDOWS = (2, 4, 8, 16)
EPS = 1e-6
SUBTILE = 256
TOKENS_PER_STEP = 512
HEADS_PER_GROUP = 2
INTRA_ROWS = 128
LOOKAHEAD = 2
ADA_COLS_PER_STEP = 512
COND_ROWS = 16
VMEM_LIMIT_BYTES = 56 * 1024 * 1024

F32 = jnp.float32
BF16 = jnp.bfloat16


def _silu(a):
    return a * jax.nn.sigmoid(a)


def _rmsnorm(xf, w):
    y = xf * lax.rsqrt(jnp.mean(xf * xf, axis=-1, keepdims=True) + EPS)
    return y * w


def _split_bf16(a):
    hi = a.astype(BF16)
    lo = (a - hi.astype(F32)).astype(BF16)
    return hi, lo


def _dot(a, b):
    return jnp.dot(a, b, preferred_element_type=F32)


def _dot_nt(a, b):
    return lax.dot_general(a, b, (((1,), (1,)), ((), ())), preferred_element_type=F32)


def _dot_tn(a, b):
    return lax.dot_general(a, b, (((0,), (0,)), ((), ())), preferred_element_type=F32)


def _lower_bound(logits, layer):
    rows = [logits[n] for n in range(logits.shape[0])]
    m = jnp.zeros_like(rows[0])
    for r in rows:
        m = jnp.maximum(m, r)
    exps = [jnp.exp(r - m) for r in rows]
    denom = jnp.exp(-m)
    for e in exps:
        denom = denom + e
    acc = exps[0]
    for e in exps[1:layer + 1]:
        acc = acc + e
    return acc / denom


def _run_pipeline(items, issue, prepare, first, second, finish_after):
    ahead = min(LOOKAHEAD, len(items))
    issued = [issue(item) for item in items[:ahead]]
    prepared = [prepare(item, handle) for item, handle in zip(items, issued)]
    deferred = []
    for i, item in enumerate(items):
        cont = first(item, prepared[i])
        later = items[i + ahead] if i + ahead < len(items) else None
        if later is not None:
            handle = issue(later)
        for fn in deferred:
            fn()
        deferred = []
        second(cont)
        fn = finish_after(item)
        if fn is not None:
            deferred.append(fn)
        if later is not None:
            prepared.append(prepare(later, handle))
    for fn in deferred:
        fn()


def _scan_head(reverse, b, k, v, q, keep, st):
    tq = b.shape[0]
    nc = tq // CHUNK
    per_block = INTRA_ROWS // CHUNK
    mid = CHUNK // 2 if reverse else CHUNK // 2 - 1
    last = 0 if reverse else CHUNK - 1
    k_dec, q_dec, k_end, q_in, decay = [], [], [], [], []
    for n in range(nc):
        rows = slice(n * CHUNK, (n + 1) * CHUNK)
        bn = b[rows]
        b_mid = bn[mid:mid + 1]
        b_last = bn[last:last + 1]
        kd = k[rows] * jnp.exp(b_mid - bn)
        k_dec.append(kd.astype(BF16))
        k_end.append((kd * jnp.exp(b_last - b_mid)).astype(BF16))
        decay.append(jnp.exp(b_last))
        if q is not None:
            qd = q[rows] * jnp.exp(bn - b_mid)
            q_dec.append(qd.astype(BF16))
            q_in.append((qd * jnp.exp(b_mid)).astype(BF16))
    scores = []
    if q is not None:
        for j in range(nc // per_block):
            blk = slice(j * per_block, (j + 1) * per_block)
            scores.append(_dot_nt(jnp.concatenate(q_dec[blk], axis=0),
                                  jnp.concatenate(k_dec[blk], axis=0)))
    update = [_dot_tn(v[n * CHUNK:(n + 1) * CHUNK], k_end[n]) for n in range(nc)]
    yield
    carried = [None] * nc
    for n in (range(nc - 1, -1, -1) if reverse else range(nc)):
        carried[n] = st.astype(BF16)
        st = st * decay[n] + update[n]
    if q is None:
        yield None, st
        return
    scores = [jnp.where(keep, a, 0.0).astype(BF16) for a in scores]
    o_intra = [_dot(a, v[j * INTRA_ROWS:(j + 1) * INTRA_ROWS]) for j, a in enumerate(scores)]
    o_inter = [_dot_nt(q_in[n], carried[n]) for n in range(nc)]
    yield jnp.concatenate(o_intra, axis=0) + jnp.concatenate(o_inter, axis=0), st


def _hgrn_items(direction, subtiles, hb_of, w_ref, col0, lg_ref, cum_ref, st_ref, emit, finish,
                shared=None):
    d = w_ref.shape[0]
    dh = d // HEADS
    gw = HEADS_PER_GROUP * dh
    n_groups = HEADS // HEADS_PER_GROUP
    reverse = direction == 1
    cum = cum_ref[...].astype(BF16)
    keep = cum_ref[0:INTRA_ROWS, 0:INTRA_ROWS] != 0.0
    lb_all = _lower_bound(lg_ref[:, direction], 0)
    hb_cache = {}

    def hb(s):
        if s not in hb_cache:
            hb_cache[s] = hb_of(s)
        return hb_cache[s]

    def issue(item):
        s, g = item
        raw = {}
        for name in col0:
            cols = slice(col0[name] + g * gw, col0[name] + (g + 1) * gw)
            if name == "i" and shared is not None:
                if item not in shared:
                    shared[item] = _dot(hb(s), w_ref[:, cols])
                raw[name] = shared[item]
            else:
                raw[name] = _dot(hb(s), w_ref[:, cols])
        return raw

    def prepare(item, raw):
        s, g = item
        lb = lb_all[:, g * gw:(g + 1) * gw]
        f = lb + (1.0 - lb) * jax.nn.sigmoid(raw["f"])
        hi, lo = _split_bf16(jnp.log(f))
        b = _dot(cum, hi) + _dot(cum, lo)
        q = _silu(raw["q"]) * (dh ** -0.5) if "q" in raw else None
        return b, 1.0 - f, raw["i"].astype(BF16), q, raw.get("z")

    def first(item, prepared):
        s, g = item
        b, k, v, q, z = prepared
        heads = []
        for hh in range(HEADS_PER_GROUP):
            lanes = slice(hh * dh, (hh + 1) * dh)
            stages = _scan_head(reverse, b[:, lanes], k[:, lanes], v[:, lanes],
                                None if q is None else q[:, lanes], keep,
                                st_ref[g * HEADS_PER_GROUP + hh])
            next(stages)
            heads.append(stages)
        return item, heads, z

    def second(cont):
        (s, g), heads, z = cont
        for hh, stages in enumerate(heads):
            h = g * HEADS_PER_GROUP + hh
            o, st = next(stages)
            st_ref[h] = st
            if emit is not None:
                emit(s, h, o, None if z is None else z[:, hh * dh:(hh + 1) * dh])

    def finish_after(item):
        s, g = item
        if finish is None or g != n_groups - 1:
            return None
        return lambda: finish(s)

    items = [(s, g) for s in subtiles for g in range(n_groups)]
    _run_pipeline(items, issue, prepare, first, second, finish_after)


def _ada_body(cond_ref, w_ref, b_ref, o_ref):
    o_ref[0] = _dot(_silu(cond_ref[...]), w_ref[0]) + b_ref[0]


def _ada_mods(cond, ada_w, ada_b):
    depth, d, n = ada_w.shape
    nb = ADA_COLS_PER_STEP
    return pl.pallas_call(
        _ada_body,
        grid=(depth, n // nb),
        in_specs=[
            pl.BlockSpec((COND_ROWS, d), lambda l, j: (0, 0)),
            pl.BlockSpec((1, d, nb), lambda l, j: (l, 0, j)),
            pl.BlockSpec((1, 1, nb), lambda l, j: (l, 0, j)),
        ],
        out_specs=pl.BlockSpec((1, COND_ROWS, nb), lambda l, j: (l, 0, j)),
        out_shape=jax.ShapeDtypeStruct((depth, COND_ROWS, n), F32),
        name="ada_mods",
    )(cond, ada_w, ada_b.reshape(depth, 1, n))


def _ctx_body(ctx_ref, shift_ref, scale_ref, nw_ref, w_ref, lg_ref, cumf_ref, cumb_ref,
              sf_ref, sb_ref):
    d = ctx_ref.shape[-1]
    subtiles = list(range(ctx_ref.shape[1] // SUBTILE))

    def hb_of(s):
        rows = slice(s * SUBTILE, (s + 1) * SUBTILE)
        hc = _rmsnorm(ctx_ref[0, rows], nw_ref[...]) * (1.0 + scale_ref[...]) + shift_ref[...]
        return hc.astype(BF16)

    shared = {}
    for direction, (cum_ref, st_ref) in enumerate(((cumf_ref, sf_ref), (cumb_ref, sb_ref))):
        st = st_ref.at[0]
        st[...] = jnp.zeros(st.shape, F32)
        _hgrn_items(direction, subtiles[::-1] if direction else subtiles, hb_of, w_ref,
                    {"f": direction * d, "i": 2 * d}, lg_ref, cum_ref, st, None, None, shared)


def _ctx_states(ctx, shift, scale, norm_w, w_ctx, lb_logits, cum_f, cum_b):
    bsz, tc, d = ctx.shape
    dh = d // HEADS
    const2 = lambda b: (0, 0)
    state_spec = pl.BlockSpec((1, HEADS, dh, dh), lambda b: (b, 0, 0, 0))
    state_shape = jax.ShapeDtypeStruct((bsz, HEADS, dh, dh), F32)
    return pl.pallas_call(
        _ctx_body,
        grid=(bsz,),
        in_specs=[
            pl.BlockSpec((1, tc, d), lambda b: (b, 0, 0)),
            pl.BlockSpec((1, d), const2),
            pl.BlockSpec((1, d), const2),
            pl.BlockSpec((1, d), const2),
            pl.BlockSpec(w_ctx.shape, const2),
            pl.BlockSpec(lb_logits.shape, lambda b: (0, 0, 0, 0)),
            pl.BlockSpec(cum_f.shape, const2),
            pl.BlockSpec(cum_b.shape, const2),
        ],
        out_specs=[state_spec, state_spec],
        out_shape=[state_shape, state_shape],
        compiler_params=pltpu.CompilerParams(vmem_limit_bytes=VMEM_LIMIT_BYTES),
        name="ctx_states",
    )(ctx, shift, scale, norm_w, w_ctx, lb_logits, cum_f, cum_b)


def _normalised_rows(x_ref, rows, shift_ref, scale_ref, nw_ref):
    hx = _rmsnorm(x_ref[0, rows], nw_ref[...]) * (1.0 + scale_ref[0]) + shift_ref[0]
    return hx.astype(BF16)


def _load_initial_state(s0_ref, st_ref):
    @pl.when(pl.program_id(1) == 0)
    def _():
        st_ref[...] = s0_ref[0]


def _subtile_rows(s):
    return slice(s * SUBTILE, (s + 1) * SUBTILE)


def _fwd_body(x_ref, shift_ref, scale_ref, nw_ref, w_ref, lg_ref, cum_ref, s0_ref,
              o_ref, st_ref):
    d = x_ref.shape[-1]
    dh = d // HEADS
    _load_initial_state(s0_ref, st_ref)

    def hb_of(s):
        return _normalised_rows(x_ref, _subtile_rows(s), shift_ref, scale_ref, nw_ref)

    def emit(s, h, o, z):
        o_ref[0, _subtile_rows(s), h * dh:(h + 1) * dh] = o

    subtiles = list(range(x_ref.shape[1] // SUBTILE))
    _hgrn_items(0, subtiles, hb_of, w_ref, {"f": 0, "i": d, "q": 2 * d}, lg_ref, cum_ref,
                st_ref, emit, None)


def _bwd_body(x_ref, shift_ref, scale_ref, gate_ref, nw_ref, w_ref, lg_ref, cum_ref, s0_ref,
              of_ref, gw_ref, wo_ref, x1_ref, st_ref):
    d = x_ref.shape[-1]
    dh = d // HEADS
    _load_initial_state(s0_ref, st_ref)
    pieces = {}

    def hb_of(s):
        return _normalised_rows(x_ref, _subtile_rows(s), shift_ref, scale_ref, nw_ref)

    def emit(s, h, o, z):
        lanes = slice(h * dh, (h + 1) * dh)
        o = _rmsnorm(of_ref[0, _subtile_rows(s), lanes] + o, gw_ref[:, lanes])
        pieces.setdefault(s, []).append((o * _silu(z)).astype(BF16))

    def finish(s):
        rows = _subtile_rows(s)
        mix = _dot(jnp.concatenate(pieces.pop(s), axis=1), wo_ref[...])
        x1_ref[0, rows] = x_ref[0, rows] + gate_ref[0] * mix

    subtiles = list(range(x_ref.shape[1] // SUBTILE))[::-1]
    _hgrn_items(1, subtiles, hb_of, w_ref, {"f": 0, "i": d, "q": 2 * d, "z": 3 * d}, lg_ref,
                cum_ref, st_ref, emit, finish)


def _hgrn_layer(x, shift, scale, gate, norm_w, w_fwd, w_bwd, lb_logits, cum_f, cum_b,
                s0_f, s0_b, gnorm_w, w_out):
    bsz, t, d = x.shape
    dh = d // HEADS
    tq = TOKENS_PER_STEP
    nt = t // tq
    params = pltpu.CompilerParams(vmem_limit_bytes=VMEM_LIMIT_BYTES)
    row = pl.BlockSpec((1, 1, d), lambda b, i: (b, 0, 0))

    def tail_specs(w, cum):
        return [
            pl.BlockSpec((1, d), lambda b, i: (0, 0)),
            pl.BlockSpec(w.shape, lambda b, i: (0, 0)),
            pl.BlockSpec(lb_logits.shape, lambda b, i: (0, 0, 0, 0)),
            pl.BlockSpec(cum.shape, lambda b, i: (0, 0)),
            pl.BlockSpec((1, HEADS, dh, dh), lambda b, i: (b, 0, 0, 0)),
        ]

    tile_f = pl.BlockSpec((1, tq, d), lambda b, i: (b, i, 0))
    o_f = pl.pallas_call(
        _fwd_body,
        grid=(bsz, nt),
        in_specs=[tile_f, row, row] + tail_specs(w_fwd, cum_f),
        out_specs=tile_f,
        out_shape=jax.ShapeDtypeStruct((bsz, t, d), F32),
        scratch_shapes=[pltpu.VMEM((HEADS, dh, dh), F32)],
        compiler_params=params,
        name="hgrn_fwd_sweep",
    )(x, shift, scale, norm_w, w_fwd, lb_logits, cum_f, s0_f)

    tile_b = pl.BlockSpec((1, tq, d), lambda b, i: (b, nt - 1 - i, 0))
    return pl.pallas_call(
        _bwd_body,
        grid=(bsz, nt),
        in_specs=[tile_b, row, row, row] + tail_specs(w_bwd, cum_b) + [
            tile_b,
            pl.BlockSpec((1, d), lambda b, i: (0, 0)),
            pl.BlockSpec((d, d), lambda b, i: (0, 0)),
        ],
        out_specs=tile_b,
        out_shape=jax.ShapeDtypeStruct((bsz, t, d), F32),
        scratch_shapes=[pltpu.VMEM((HEADS, dh, dh), F32)],
        compiler_params=params,
        name="hgrn_bwd_sweep",
    )(x, shift, scale, gate, norm_w, w_bwd, lb_logits, cum_b, s0_b, o_f, gnorm_w, w_out)


def _pool_body(x_ref, shift_ref, scale_ref, gate_ref, nw_ref, wi_ref, band_ref, inv_ref, wg_ref,
               ps_ref, wo_ref, fw_ref, out_ref):
    d = x_ref.shape[-1]
    gd = d // len(POOL_WINDOWS)
    hb_cache = {}
    pieces = {}

    def hb(s):
        if s not in hb_cache:
            hb_cache[s] = _normalised_rows(x_ref, _subtile_rows(s), shift_ref, scale_ref, nw_ref)
        return hb_cache[s]

    def issue(item):
        s, g = item
        u = _dot(hb(s), wi_ref[:, g * gd:(g + 1) * gd])
        z = _dot(hb(s), wi_ref[:, d + g * gd:d + (g + 1) * gd])
        return u, z

    def prepare(item, handle):
        u, z = handle
        return (u, z) + _split_bf16(u)

    def first(item, prepared):
        s, g = item
        u, z, u_hi, u_lo = prepared
        band = band_ref[g]
        return item, u, z, _dot(band, u_hi) + _dot(band, u_lo)

    def second(cont):
        (s, g), u, z, window_sum = cont
        lanes = slice(g * gd, (g + 1) * gd)
        pooled = window_sum * inv_ref[g] - u
        y = _dot(pooled.astype(BF16), wg_ref[g]) * ps_ref[:, lanes]
        pieces.setdefault(s, []).append((y * _silu(z)).astype(BF16))

    def finish_after(item):
        s, g = item
        if g != len(POOL_WINDOWS) - 1:
            return None

        def finish():
            rows = _subtile_rows(s)
            mix = _dot(jnp.concatenate(pieces.pop(s), axis=1), wo_ref[...])
            out_ref[0, rows] = _rmsnorm(x_ref[0, rows] + gate_ref[0] * mix, fw_ref[...])

        return finish

    items = [(s, g) for s in range(x_ref.shape[1] // SUBTILE) for g in range(len(POOL_WINDOWS))]
    _run_pipeline(items, issue, prepare, first, second, finish_after)


def _pool_layer(x, shift, scale, gate, norm_w, w_in, band, inv_count, w_grp, pool_scale, w_out,
                final_w):
    bsz, t, d = x.shape
    tq = TOKENS_PER_STEP
    tile = pl.BlockSpec((1, tq, d), lambda b, i: (b, i, 0))
    row = pl.BlockSpec((1, 1, d), lambda b, i: (b, 0, 0))
    vec = pl.BlockSpec((1, d), lambda b, i: (0, 0))
    whole3 = lambda a: pl.BlockSpec(a.shape, lambda b, i: (0, 0, 0))
    return pl.pallas_call(
        _pool_body,
        grid=(bsz, t // tq),
        in_specs=[
            tile, row, row, row, vec,
            pl.BlockSpec(w_in.shape, lambda b, i: (0, 0)),
            whole3(band), whole3(inv_count), whole3(w_grp),
            vec,
            pl.BlockSpec(w_out.shape, lambda b, i: (0, 0)),
            vec,
        ],
        out_specs=tile,
        out_shape=jax.ShapeDtypeStruct((bsz, t, d), F32),
        compiler_params=pltpu.CompilerParams(vmem_limit_bytes=VMEM_LIMIT_BYTES),
        name="pool_layer",
    )(x, shift, scale, gate, norm_w, w_in, band, inv_count, w_grp, pool_scale, w_out, final_w)


def _cumulative_matrices(n):
    r = jnp.arange(n)[:, None]
    c = jnp.arange(n)[None, :]
    same = (r // CHUNK) == (c // CHUNK)
    return (same & (c <= r)).astype(F32), (same & (c >= r)).astype(F32)


def _pool_windows(n):
    r = jnp.arange(n)[:, None]
    c = jnp.arange(n)[None, :]
    same = (r // GRID_W) == (c // GRID_W)
    band = jnp.stack([(same & (c >= r - w // 2) & (c < r - w // 2 + w)) for w in POOL_WINDOWS])
    count = jnp.sum(band.astype(F32), axis=-1, keepdims=True)
    return band.astype(BF16), 1.0 / count


def kernel(x, c, ctx, c_ctx, ada_w, ada_b, norm_w, hgrn_w_in, hgrn_lb_logits, hgrn_gnorm_w,
           hgrn_w_out, pool_w_in, pool_w_grp, pool_scale, pool_w_out, final_norm_w):
    bsz, t, d = x.shape
    depth = ada_w.shape[0]
    assert depth == 2 and hgrn_w_in.shape[0] == 1 and pool_w_in.shape[0] == 1
    assert t % TOKENS_PER_STEP == 0 and ctx.shape[1] % SUBTILE == 0 and bsz < COND_ROWS
    assert TOKENS_PER_STEP % SUBTILE == 0 and SUBTILE % INTRA_ROWS == 0
    assert INTRA_ROWS % CHUNK == 0 and SUBTILE % GRID_W == 0

    cond = jnp.zeros((COND_ROWS, d), F32).at[:bsz].set(c).at[bsz].set(c_ctx)
    mods = _ada_mods(cond, ada_w, ada_b)
    shift, scale, gate = (mods[:, :, i * d:(i + 1) * d] for i in range(3))
    per_batch = lambda m, l: m[l, :bsz].reshape(bsz, 1, d)

    w_in = hgrn_w_in[0].astype(BF16)
    w_ctx = w_in[:, :3 * d]
    w_fwd = jnp.concatenate([w_in[:, :d], w_in[:, 2 * d:4 * d]], axis=1)
    w_bwd = w_in[:, d:]
    lb_logits = hgrn_lb_logits.reshape(hgrn_lb_logits.shape[0], 2, 1, d)
    cum_f, cum_b = _cumulative_matrices(SUBTILE)
    band, inv_count = _pool_windows(SUBTILE)

    s0_f, s0_b = _ctx_states(ctx, shift[0, bsz:bsz + 1], scale[0, bsz:bsz + 1], norm_w[0:1],
                             w_ctx, lb_logits, cum_f, cum_b)
    x1 = _hgrn_layer(x, per_batch(shift, 0), per_batch(scale, 0), per_batch(gate, 0),
                     norm_w[0:1], w_fwd, w_bwd, lb_logits, cum_f, cum_b, s0_f, s0_b,
                     hgrn_gnorm_w[0:1], hgrn_w_out[0].astype(BF16))
    return _pool_layer(x1, per_batch(shift, 1), per_batch(scale, 1), per_batch(gate, 1),
                       norm_w[1:2], pool_w_in[0].astype(BF16), band, inv_count,
                       pool_w_grp[0].astype(BF16), pool_scale[0:1], pool_w_out[0].astype(BF16),
                       final_norm_w.reshape(1, d))
```

```python
import jax
import jax.numpy as jnp
from jax import lax
from jax.experimental import pallas as pl
from jax.experimental.pallas import tpu as pltpu

HEADS = 8
CHUNK = 64
GRID_W = 64
POOL_WINDOWS = (2, 4, 8, 16)
EPS = 1e-6
SUBTILE = 256
TOKENS_PER_STEP = 512
HEADS_PER_GROUP = 2
INTRA_ROWS = 128
LOOKAHEAD = 2
ADA_COLS_PER_STEP = 512
COND_ROWS = 16
VMEM_LIMIT_BYTES = 56 * 1024 * 1024

F32 = jnp.float32
BF16 = jnp.bfloat16


def _silu(a):
    return a * jax.nn.sigmoid(a)


def _rmsnorm(xf, w):
    y = xf * lax.rsqrt(jnp.mean(xf * xf, axis=-1, keepdims=True) + EPS)
    return y * w


def _split_bf16(a):
    hi = a.astype(BF16)
    lo = (a - hi.astype(F32)).astype(BF16)
    return hi, lo


def _dot(a, b):
    return jnp.dot(a, b, preferred_element_type=F32)


def _dot_nt(a, b):
    return lax.dot_general(a, b, (((1,), (1,)), ((), ())), preferred_element_type=F32)


def _dot_tn(a, b):
    return lax.dot_general(a, b, (((0,), (0,)), ((), ())), preferred_element_type=F32)


def _lower_bound(logits, layer):
    rows = [logits[n] for n in range(logits.shape[0])]
    m = jnp.zeros_like(rows[0])
    for r in rows:
        m = jnp.maximum(m, r)
    exps = [jnp.exp(r - m) for r in rows]
    denom = jnp.exp(-m)
    for e in exps:
        denom = denom + e
    acc = exps[0]
    for e in exps[1:layer + 1]:
        acc = acc + e
    return acc / denom


def _subtile_rows(s):
    return slice(s * SUBTILE, (s + 1) * SUBTILE)


def _pipeline(items, issue, prepare, first, second, finish_after):
    ahead = min(LOOKAHEAD, len(items))
    issued = []
    for item in items[:ahead]:
        issued.append(issue(item))
        yield
    prepared = []
    for item, handle in zip(items, issued):
        prepared.append(prepare(item, handle))
        yield
    deferred = []
    for i, item in enumerate(items):
        cont = first(item, prepared[i])
        yield
        later = items[i + ahead] if i + ahead < len(items) else None
        if later is not None:
            handle = issue(later)
            yield
        for fn in deferred:
            fn()
            yield
        deferred = []
        second(cont)
        yield
        fn = finish_after(item)
        if fn is not None:
            deferred.append(fn)
        if later is not None:
            prepared.append(prepare(later, handle))
            yield
    for fn in deferred:
        fn()
        yield


def _interleave(*streams):
    active = list(streams)
    while active:
        for stream in list(active):
            try:
                next(stream)
            except StopIteration:
                active.remove(stream)


def _scan_head(reverse, b, k, v, q, keep, st):
    tq = b.shape[0]
    nc = tq // CHUNK
    per_block = INTRA_ROWS // CHUNK
    mid = CHUNK // 2 if reverse else CHUNK // 2 - 1
    last = 0 if reverse else CHUNK - 1
    k_dec, q_dec, k_end, q_in, decay = [], [], [], [], []
    for n in range(nc):
        rows = slice(n * CHUNK, (n + 1) * CHUNK)
        bn = b[rows]
        b_mid = bn[mid:mid + 1]
        b_last = bn[last:last + 1]
        kd = k[rows] * jnp.exp(b_mid - bn)
        k_dec.append(kd.astype(BF16))
        k_end.append((kd * jnp.exp(b_last - b_mid)).astype(BF16))
        decay.append(jnp.exp(b_last))
        if q is not None:
            qd = q[rows] * jnp.exp(bn - b_mid)
            q_dec.append(qd.astype(BF16))
            q_in.append((qd * jnp.exp(b_mid)).astype(BF16))
    scores = []
    if q is not None:
        for j in range(nc // per_block):
            blk = slice(j * per_block, (j + 1) * per_block)
            scores.append(_dot_nt(jnp.concatenate(q_dec[blk], axis=0),
                                  jnp.concatenate(k_dec[blk], axis=0)))
    update = [_dot_tn(v[n * CHUNK:(n + 1) * CHUNK], k_end[n]) for n in range(nc)]
    yield
    carried = [None] * nc
    for n in (range(nc - 1, -1, -1) if reverse else range(nc)):
        carried[n] = st.astype(BF16)
        st = st * decay[n] + update[n]
    if q is None:
        yield None, st
        return
    scores = [jnp.where(keep, a, 0.0).astype(BF16) for a in scores]
    o_intra = [_dot(a, v[j * INTRA_ROWS:(j + 1) * INTRA_ROWS]) for j, a in enumerate(scores)]
    o_inter = [_dot_nt(q_in[n], carried[n]) for n in range(nc)]
    yield jnp.concatenate(o_intra, axis=0) + jnp.concatenate(o_inter, axis=0), st


def _hgrn_items(direction, subtiles, fetch, lg_ref, cum_ref, st_ref, emit):
    dh = st_ref.shape[-1]
    gw = HEADS_PER_GROUP * dh
    reverse = direction == 1
    cum = cum_ref[...].astype(BF16)
    keep = cum_ref[0:INTRA_ROWS, 0:INTRA_ROWS] != 0.0
    lb_all = _lower_bound(lg_ref[:, direction], 0)

    def issue(item):
        return fetch(*item)

    def prepare(item, fetched):
        s, g = item
        f_pre, v, q = fetched
        lb = lb_all[:, g * gw:(g + 1) * gw]
        f = lb + (1.0 - lb) * jax.nn.sigmoid(f_pre)
        hi, lo = _split_bf16(jnp.log(f))
        return _dot(cum, hi) + _dot(cum, lo), 1.0 - f, v, q

    def first(item, prepared):
        s, g = item
        b, k, v, q = prepared
        heads = []
        for hh in range(HEADS_PER_GROUP):
            lanes = slice(hh * dh, (hh + 1) * dh)
            stages = _scan_head(reverse, b[:, lanes], k[:, lanes], v[:, lanes],
                                None if q is None else q[:, lanes], keep,
                                st_ref[g * HEADS_PER_GROUP + hh])
            next(stages)
            heads.append(stages)
        return item, heads

    def second(cont):
        (s, g), heads = cont
        for hh, stages in enumerate(heads):
            h = g * HEADS_PER_GROUP + hh
            o, st = next(stages)
            st_ref[h] = st
            if emit is not None:
                emit(s, h, o)

    items = [(s, g) for s in subtiles for g in range(HEADS // HEADS_PER_GROUP)]
    return _pipeline(items, issue, prepare, first, second, lambda item: None)


def _ada_body(cond_ref, w_ref, b_ref, o_ref):
    o_ref[0] = _dot(_silu(cond_ref[...]), w_ref[0]) + b_ref[0]


def _ada_mods(cond, ada_w, ada_b):
    depth, d, n = ada_w.shape
    nb = ADA_COLS_PER_STEP
    return pl.pallas_call(
        _ada_body,
        grid=(depth, n // nb),
        in_specs=[
            pl.BlockSpec((COND_ROWS, d), lambda l, j: (0, 0)),
            pl.BlockSpec((1, d, nb), lambda l, j: (l, 0, j)),
            pl.BlockSpec((1, 1, nb), lambda l, j: (l, 0, j)),
        ],
        out_specs=pl.BlockSpec((1, COND_ROWS, nb), lambda l, j: (l, 0, j)),
        out_shape=jax.ShapeDtypeStruct((depth, COND_ROWS, n), F32),
        name="ada_mods",
    )(cond, ada_w, ada_b.reshape(depth, 1, n))


def _ctx_body(ctx_ref, shift_ref, scale_ref, nw_ref, w_ref, lg_ref, cumf_ref, cumb_ref,
              sf_ref, sb_ref):
    d = ctx_ref.shape[-1]
    gw = HEADS_PER_GROUP * (d // HEADS)
    subtiles = list(range(ctx_ref.shape[1] // SUBTILE))
    hb, v = {}, {}

    def fetch(direction, s, g):
        if s not in hb:
            hc = _rmsnorm(ctx_ref[0, _subtile_rows(s)], nw_ref[...])
            hb[s] = (hc * (1.0 + scale_ref[...]) + shift_ref[...]).astype(BF16)
        cols = lambda block: slice(block * d + g * gw, block * d + (g + 1) * gw)
        if (s, g) not in v:
            v[s, g] = _dot(hb[s], w_ref[:, cols(2)]).astype(BF16)
        return _dot(hb[s], w_ref[:, cols(direction)]), v[s, g], None

    for direction, (cum_ref, st_ref) in enumerate(((cumf_ref, sf_ref), (cumb_ref, sb_ref))):
        st = st_ref.at[0]
        st[...] = jnp.zeros(st.shape, F32)
        order = subtiles[::-1] if direction else subtiles
        for _ in _hgrn_items(direction, order, lambda s, g, dr=direction: fetch(dr, s, g),
                             lg_ref, cum_ref, st, None):
            pass


def _ctx_states(ctx, shift, scale, norm_w, w_ctx, lb_logits, cum_f, cum_b):
    bsz, tc, d = ctx.shape
    dh = d // HEADS
    const2 = lambda b: (0, 0)
    state_spec = pl.BlockSpec((1, HEADS, dh, dh), lambda b: (b, 0, 0, 0))
    state_shape = jax.ShapeDtypeStruct((bsz, HEADS, dh, dh), F32)
    return pl.pallas_call(
        _ctx_body,
        grid=(bsz,),
        in_specs=[
            pl.BlockSpec((1, tc, d), lambda b: (b, 0, 0)),
            pl.BlockSpec((1, d), const2),
            pl.BlockSpec((1, d), const2),
            pl.BlockSpec((1, d), const2),
            pl.BlockSpec(w_ctx.shape, const2),
            pl.BlockSpec(lb_logits.shape, lambda b: (0, 0, 0, 0)),
            pl.BlockSpec(cum_f.shape, const2),
            pl.BlockSpec(cum_b.shape, const2),
        ],
        out_specs=[state_spec, state_spec],
        out_shape=[state_shape, state_shape],
        compiler_params=pltpu.CompilerParams(vmem_limit_bytes=VMEM_LIMIT_BYTES),
        name="ctx_states",
    )(ctx, shift, scale, norm_w, w_ctx, lb_logits, cum_f, cum_b)


def _normalised_rows(x, shift_ref, scale_ref, nw_ref):
    return (_rmsnorm(x, nw_ref[...]) * (1.0 + scale_ref[0]) + shift_ref[0]).astype(BF16)


def _load_initial_state(s0_ref, st_ref):
    @pl.when(pl.program_id(1) == 0)
    def _():
        st_ref[...] = s0_ref[0]


def _fwd_body(x_ref, shift_ref, scale_ref, nw_ref, w_ref, lg_ref, cum_ref, s0_ref,
              of_ref, fb_ref, v_ref, q_ref, zg_ref, st_ref):
    d = x_ref.shape[-1]
    dh = d // HEADS
    gw = HEADS_PER_GROUP * dh
    _load_initial_state(s0_ref, st_ref)
    hb = {}

    def fetch(s, g):
        rows = _subtile_rows(s)
        if s not in hb:
            hb[s] = _normalised_rows(x_ref[0, rows], shift_ref, scale_ref, nw_ref)
        lanes = slice(g * gw, (g + 1) * gw)
        f_fwd, f_bwd, i, q, z = (_dot(hb[s], w_ref[:, block * d + g * gw:block * d + (g + 1) * gw])
                                 for block in range(5))
        v = i.astype(BF16)
        q = _silu(q) * (dh ** -0.5)
        fb_ref[0, rows, lanes] = f_bwd.astype(BF16)
        v_ref[0, rows, lanes] = v
        q_ref[0, rows, lanes] = q.astype(BF16)
        zg_ref[0, rows, lanes] = _silu(z).astype(BF16)
        return f_fwd, v, q

    def emit(s, h, o):
        of_ref[0, _subtile_rows(s), h * dh:(h + 1) * dh] = o.astype(BF16)

    subtiles = list(range(x_ref.shape[1] // SUBTILE))
    for _ in _hgrn_items(0, subtiles, fetch, lg_ref, cum_ref, st_ref, emit):
        pass


def _fwd_sweep(x, shift, scale, norm_w, w_in, lb_logits, cum_f, s0_f):
    bsz, t, d = x.shape
    dh = d // HEADS
    tq = TOKENS_PER_STEP
    tile = pl.BlockSpec((1, tq, d), lambda b, i: (b, i, 0))
    row = pl.BlockSpec((1, 1, d), lambda b, i: (b, 0, 0))
    saved = jax.ShapeDtypeStruct((bsz, t, d), BF16)
    return pl.pallas_call(
        _fwd_body,
        grid=(bsz, t // tq),
        in_specs=[
            tile, row, row,
            pl.BlockSpec((1, d), lambda b, i: (0, 0)),
            pl.BlockSpec(w_in.shape, lambda b, i: (0, 0)),
            pl.BlockSpec(lb_logits.shape, lambda b, i: (0, 0, 0, 0)),
            pl.BlockSpec(cum_f.shape, lambda b, i: (0, 0)),
            pl.BlockSpec((1, HEADS, dh, dh), lambda b, i: (b, 0, 0, 0)),
        ],
        out_specs=[tile] * 5,
        out_shape=[saved] * 5,
        scratch_shapes=[pltpu.VMEM((HEADS, dh, dh), F32)],
        compiler_params=pltpu.CompilerParams(vmem_limit_bytes=VMEM_LIMIT_BYTES),
        name="hgrn_fwd_sweep",
    )(x, shift, scale, norm_w, w_in, lb_logits, cum_f, s0_f)


def _pool_stream(x1, shift_ref, scale_ref, gate_ref, nw_ref, wi_ref, band_ref, inv_ref, wg_ref,
                 ps_ref, wo_ref, fw_ref, store):
    d = x1.shape[-1]
    gd = d // len(POOL_WINDOWS)
    hb = _normalised_rows(x1, shift_ref, scale_ref, nw_ref)
    pieces = []

    def issue(g):
        return (_dot(hb, wi_ref[:, g * gd:(g + 1) * gd]),
                _dot(hb, wi_ref[:, d + g * gd:d + (g + 1) * gd]))

    def prepare(g, projected):
        u, z = projected
        return (u, z) + _split_bf16(u)

    def first(g, prepared):
        u, z, u_hi, u_lo = prepared
        band = band_ref[g]
        return g, u, z, _dot(band, u_hi) + _dot(band, u_lo)

    def second(cont):
        g, u, z, window_sum = cont
        pooled = window_sum * inv_ref[g] - u
        y = _dot(pooled.astype(BF16), wg_ref[g]) * ps_ref[:, g * gd:(g + 1) * gd]
        pieces.append((y * _silu(z)).astype(BF16))

    def finish_after(g):
        if g != len(POOL_WINDOWS) - 1:
            return None

        def finish():
            mix = _dot(jnp.concatenate(pieces, axis=1), wo_ref[...])
            store(_rmsnorm(x1 + gate_ref[0] * mix, fw_ref[...]))

        return finish

    return _pipeline(list(range(len(POOL_WINDOWS))), issue, prepare, first, second, finish_after)


def _bwd_body(x_ref, of_ref, fb_ref, v_ref, q_ref, zg_ref, gate0_ref, lg_ref, cum_ref, s0_ref,
              gw_ref, wo_ref, shift1_ref, scale1_ref, gate1_ref, nw1_ref, wi_ref, band_ref,
              inv_ref, wg_ref, ps_ref, wo1_ref, fw_ref, out_ref, st_ref):
    d = x_ref.shape[-1]
    dh = d // HEADS
    gw = HEADS_PER_GROUP * dh
    _load_initial_state(s0_ref, st_ref)
    pieces = {}

    def fetch(s, g):
        at = (0, _subtile_rows(s), slice(g * gw, (g + 1) * gw))
        return fb_ref[at].astype(F32), v_ref[at], q_ref[at].astype(F32)

    def emit(s, h, o):
        at = (0, _subtile_rows(s), slice(h * dh, (h + 1) * dh))
        o = _rmsnorm(of_ref[at].astype(F32) + o, gw_ref[:, at[2]])
        pieces.setdefault(s, []).append((o * zg_ref[at].astype(F32)).astype(BF16))

    def recurrence(s):
        return _hgrn_items(1, [s], fetch, lg_ref, cum_ref, st_ref, emit)

    def rest_of_layers(s):
        rows = _subtile_rows(s)
        mix = _dot(jnp.concatenate(pieces.pop(s), axis=1), wo_ref[...])
        x1 = x_ref[0, rows] + gate0_ref[0] * mix
        yield

        def store(out):
            out_ref[0, rows] = out

        yield from _pool_stream(x1, shift1_ref, scale1_ref, gate1_ref, nw1_ref, wi_ref, band_ref,
                                inv_ref, wg_ref, ps_ref, wo1_ref, fw_ref, store)

    subtiles = list(range(x_ref.shape[1] // SUBTILE))[::-1]
    _interleave(recurrence(subtiles[0]))
    for done, nxt in zip(subtiles, subtiles[1:]):
        _interleave(rest_of_layers(done), recurrence(nxt))
    _interleave(rest_of_layers(subtiles[-1]))


def _bwd_sweep(x, saved, gate0, lb_logits, cum_b, s0_b, gnorm_w, w_out, shift1, scale1, gate1,
               norm_w1, pool_w_in, band, inv_count, w_grp, pool_scale, pool_w_out, final_w):
    bsz, t, d = x.shape
    dh = d // HEADS
    tq = TOKENS_PER_STEP
    nt = t // tq
    tile = pl.BlockSpec((1, tq, d), lambda b, i: (b, nt - 1 - i, 0))
    row = pl.BlockSpec((1, 1, d), lambda b, i: (b, 0, 0))
    vec = pl.BlockSpec((1, d), lambda b, i: (0, 0))
    whole = lambda a: pl.BlockSpec(a.shape, lambda b, i: (0,) * a.ndim)
    return pl.pallas_call(
        _bwd_body,
        grid=(bsz, nt),
        in_specs=[tile] * 6 + [
            row, whole(lb_logits), whole(cum_b),
            pl.BlockSpec((1, HEADS, dh, dh), lambda b, i: (b, 0, 0, 0)),
            vec, whole(w_out), row, row, row, vec, whole(pool_w_in), whole(band),
            whole(inv_count), whole(w_grp), vec, whole(pool_w_out), vec,
        ],
        out_specs=tile,
        out_shape=jax.ShapeDtypeStruct((bsz, t, d), F32),
        scratch_shapes=[pltpu.VMEM((HEADS, dh, dh), F32)],
        compiler_params=pltpu.CompilerParams(vmem_limit_bytes=VMEM_LIMIT_BYTES),
        name="hgrn_bwd_pool_sweep",
    )(x, *saved, gate0, lb_logits, cum_b, s0_b, gnorm_w, w_out, shift1, scale1, gate1, norm_w1,
      pool_w_in, band, inv_count, w_grp, pool_scale, pool_w_out, final_w)


def _cumulative_matrices(n):
    r = jnp.arange(n)[:, None]
    c = jnp.arange(n)[None, :]
    same = (r // CHUNK) == (c // CHUNK)
    return (same & (c <= r)).astype(F32), (same & (c >= r)).astype(F32)


def _pool_windows(n):
    r = jnp.arange(n)[:, None]
    c = jnp.arange(n)[None, :]
    same = (r // GRID_W) == (c // GRID_W)
    band = jnp.stack([(same & (c >= r - w // 2) & (c < r - w // 2 + w)) for w in POOL_WINDOWS])
    count = jnp.sum(band.astype(F32), axis=-1, keepdims=True)
    return band.astype(BF16), 1.0 / count


def kernel(x, c, ctx, c_ctx, ada_w, ada_b, norm_w, hgrn_w_in, hgrn_lb_logits, hgrn_gnorm_w,
           hgrn_w_out, pool_w_in, pool_w_grp, pool_scale, pool_w_out, final_norm_w):
    bsz, t, d = x.shape
    depth = ada_w.shape[0]
    assert depth == 2 and hgrn_w_in.shape[0] == 1 and pool_w_in.shape[0] == 1
    assert t % TOKENS_PER_STEP == 0 and ctx.shape[1] % SUBTILE == 0 and bsz < COND_ROWS
    assert TOKENS_PER_STEP % SUBTILE == 0 and SUBTILE % INTRA_ROWS == 0
    assert INTRA_ROWS % CHUNK == 0 and SUBTILE % GRID_W == 0

    cond = jnp.zeros((COND_ROWS, d), F32).at[:bsz].set(c).at[bsz].set(c_ctx)
    mods = _ada_mods(cond, ada_w, ada_b)
    shift, scale, gate = (mods[:, :, i * d:(i + 1) * d] for i in range(3))
    per_batch = lambda m, l: m[l, :bsz].reshape(bsz, 1, d)

    w_in = hgrn_w_in[0].astype(BF16)
    lb_logits = hgrn_lb_logits.reshape(hgrn_lb_logits.shape[0], 2, 1, d)
    cum_f, cum_b = _cumulative_matrices(SUBTILE)
    band, inv_count = _pool_windows(SUBTILE)

    s0_f, s0_b = _ctx_states(ctx, shift[0, bsz:bsz + 1], scale[0, bsz:bsz + 1], norm_w[0:1],
                             w_in[:, :3 * d], lb_logits, cum_f, cum_b)
    saved = _fwd_sweep(x, per_batch(shift, 0), per_batch(scale, 0), norm_w[0:1], w_in,
                       lb_logits, cum_f, s0_f)
    return _bwd_sweep(x, saved, per_batch(gate, 0), lb_logits, cum_b, s0_b, hgrn_gnorm_w[0:1],
                      hgrn_w_out[0].astype(BF16), per_batch(shift, 1), per_batch(scale, 1),
                      per_batch(gate, 1), norm_w[1:2], pool_w_in[0].astype(BF16), band, inv_count,
                      pool_w_grp[0].astype(BF16), pool_scale[0:1], pool_w_out[0].astype(BF16),
                      final_norm_w.reshape(1, d))
```

```python
import jax
import jax.numpy as jnp
from jax import lax
from jax.experimental import pallas as pl
from jax.experimental.pallas import tpu as pltpu

HEADS = 8
CHUNK = 64
GRID_W = 64
POOL_WINDOWS = (2, 4, 8, 16)
EPS = 1e-6
MASK_ROWS = 256
FWD_SUBTILE = 512
BWD_SUBTILE = 256
TOKENS_PER_STEP = 512
HEADS_PER_GROUP = 2
INTRA_ROWS = 128
LOOKAHEAD = 2
STREAM_LAG = 12
ADA_COLS_PER_STEP = 512
COND_ROWS = 16
VMEM_LIMIT_BYTES = 56 * 1024 * 1024

F32 = jnp.float32
BF16 = jnp.bfloat16

_COMPILER_PARAMS = pltpu.CompilerParams(vmem_limit_bytes=VMEM_LIMIT_BYTES)


def _silu(a):
    half = 0.5 * a
    return half + half * jnp.tanh(half)


def _rmsnorm(xf, w):
    y = xf * lax.rsqrt(jnp.mean(xf * xf, axis=-1, keepdims=True) + EPS)
    return y * w


def _split_bf16(a):
    hi = a.astype(BF16)
    lo = (a - hi.astype(F32)).astype(BF16)
    return hi, lo


def _dot(a, b):
    return jnp.dot(a, b, preferred_element_type=F32)


def _dot_nt(a, b):
    return lax.dot_general(a, b, (((1,), (1,)), ((), ())), preferred_element_type=F32)


def _dot_tn(a, b):
    return lax.dot_general(a, b, (((0,), (0,)), ((), ())), preferred_element_type=F32)


def _lower_bound(logits, layer):
    rows = [logits[n] for n in range(logits.shape[0])]
    m = jnp.zeros_like(rows[0])
    for r in rows:
        m = jnp.maximum(m, r)
    exps = [jnp.exp(r - m) for r in rows]
    denom = jnp.exp(-m)
    for e in exps:
        denom = denom + e
    acc = exps[0]
    for e in exps[1:layer + 1]:
        acc = acc + e
    return acc / denom


def _subtile_rows(s, size):
    return slice(s * size, (s + 1) * size)


def _by_mask_rows(fn, *arrays):
    n = arrays[0].shape[0] // MASK_ROWS
    blocks = [fn(*(a[i * MASK_ROWS:(i + 1) * MASK_ROWS] for a in arrays)) for i in range(n)]
    return blocks[0] if n == 1 else jnp.concatenate(blocks, axis=0)


def _resident(a):
    return pl.BlockSpec(a.shape, lambda b, i: (0,) * a.ndim, pipeline_mode=pl.Buffered(1))


class _Steps:
    def __init__(self, gen):
        self._gen = gen
        self.done = False
        self.value = None

    def step(self):
        if not self.done:
            try:
                next(self._gen)
            except StopIteration as stop:
                self.done = True
                self.value = stop.value
        return not self.done


def _alongside(main, side):
    while main.step():
        yield
        if side.step():
            yield
    return main.value


def _pipeline(items, issue, prepare, first, second, finish_after):
    ahead = min(LOOKAHEAD, len(items))
    issued = []
    for item in items[:ahead]:
        issued.append((yield from issue(item)))
    prepared = []
    for item, handle in zip(items, issued):
        prepared.append(prepare(item, handle))
        yield
    deferred = []
    for i, item in enumerate(items):
        later = items[i + ahead] if i + ahead < len(items) else None
        issuing = _Steps(issue(later) if later is not None else iter(()))
        cont = yield from _alongside(_Steps(first(item, prepared[i])), issuing)
        for fn in deferred:
            fn()
            yield
        deferred = []
        yield from _alongside(_Steps(second(cont)), issuing)
        while issuing.step():
            yield
        fn = finish_after(item)
        if fn is not None:
            deferred.append(fn)
        if later is not None:
            prepared.append(prepare(later, issuing.value))
            yield
    for fn in deferred:
        fn()
        yield


def _interleave(*streams, lag=0):
    done = [False] * len(streams)
    rnd = 0
    while not all(done):
        for k, stream in enumerate(streams):
            if done[k] or rnd < k * lag:
                continue
            try:
                next(stream)
            except StopIteration:
                done[k] = True
        rnd += 1


def _scan_head(reverse, b, k, v, q, keep, st):
    tq = b.shape[0]
    nc = tq // CHUNK
    per_block = INTRA_ROWS // CHUNK
    mid = CHUNK // 2 if reverse else CHUNK // 2 - 1
    last = 0 if reverse else CHUNK - 1
    k_dec, q_dec, k_end, q_in, decay = [], [], [], [], []
    for n in range(nc):
        rows = slice(n * CHUNK, (n + 1) * CHUNK)
        bn = b[rows]
        b_mid = bn[mid:mid + 1]
        b_last = bn[last:last + 1]
        kd = k[rows] * jnp.exp2(b_mid - bn)
        k_dec.append(kd.astype(BF16))
        k_end.append((kd * jnp.exp2(b_last - b_mid)).astype(BF16))
        decay.append(jnp.exp2(b_last))
        if q is not None:
            qd = q[rows] * jnp.exp2(bn - b_mid)
            q_dec.append(qd.astype(BF16))
            q_in.append((qd * jnp.exp2(b_mid)).astype(BF16))
    scores = []
    if q is not None:
        for j in range(nc // per_block):
            blk = slice(j * per_block, (j + 1) * per_block)
            scores.append(_dot_nt(jnp.concatenate(q_dec[blk], axis=0),
                                  jnp.concatenate(k_dec[blk], axis=0)))
        yield
    update = [_dot_tn(v[n * CHUNK:(n + 1) * CHUNK], k_end[n]) for n in range(nc)]
    yield
    if q is not None:
        scores = [jnp.where(keep, a, 0.0).astype(BF16) for a in scores]
        o_intra = [_dot(a, v[j * INTRA_ROWS:(j + 1) * INTRA_ROWS]) for j, a in enumerate(scores)]
        yield
    carried = [None] * nc
    for n in (range(nc - 1, -1, -1) if reverse else range(nc)):
        carried[n] = st.astype(BF16)
        st = st * decay[n] + update[n]
    if q is None:
        return None, st
    o_inter = [_dot_nt(q_in[n], carried[n]) for n in range(nc)]
    return jnp.concatenate(o_intra, axis=0) + jnp.concatenate(o_inter, axis=0), st


class _StateOrder:
    def __init__(self, subtiles):
        self._before = {s: subtiles[:i] for i, s in enumerate(subtiles)}
        self._written = set()

    def read(self, s, h):
        missing = [p for p in self._before[s] if (p, h) not in self._written]
        assert not missing, f"state of head {h} read for subtile {s} before {missing} wrote it"

    def wrote(self, s, h):
        self._written.add((s, h))


def _hgrn_items(direction, subtiles, fetch, lg_ref, masks, st_ref, emit, order=None):
    dh = st_ref.shape[-1]
    gw = HEADS_PER_GROUP * dh
    reverse = direction == 1
    cum_ref, keep_ref = masks
    cum = cum_ref[...]
    keep = keep_ref[...] != 0.0
    lb_all = _lower_bound(lg_ref[:, direction], 0)

    def issue(item):
        return (yield from fetch(*item))

    def prepare(item, fetched):
        s, g = item
        f_pre, v, q = fetched
        lb = lb_all[:, g * gw:(g + 1) * gw]
        f = 0.5 * (1.0 + lb) + (0.5 * (1.0 - lb)) * jnp.tanh(0.5 * f_pre)
        b = _by_mask_rows(lambda hi, lo: _dot(cum, hi) + _dot(cum, lo), *_split_bf16(jnp.log2(f)))
        return b, 1.0 - f, v, q

    def first(item, prepared):
        s, g = item
        b, k, v, q = prepared
        heads = []
        for hh in range(HEADS_PER_GROUP):
            lanes = slice(hh * dh, (hh + 1) * dh)
            if order is not None:
                order.read(s, g * HEADS_PER_GROUP + hh)
            head = _Steps(_scan_head(reverse, b[:, lanes], k[:, lanes], v[:, lanes],
                                     None if q is None else q[:, lanes], keep,
                                     st_ref[g * HEADS_PER_GROUP + hh]))
            for _ in range(1 if q is None else 2):
                head.step()
                yield
            heads.append(head)
        return item, heads

    def second(cont):
        (s, g), heads = cont
        for hh, head in enumerate(heads):
            h = g * HEADS_PER_GROUP + hh
            while head.step():
                yield
            o, st = head.value
            st_ref[h] = st
            if order is not None:
                order.wrote(s, h)
            if emit is not None:
                emit(s, h, o)
            yield

    items = [(s, g) for s in subtiles for g in range(HEADS // HEADS_PER_GROUP)]
    return _pipeline(items, issue, prepare, first, second, lambda item: None)


def _ada_body(cond_ref, w_ref, b_ref, o_ref):
    o_ref[0] = _dot(_silu(cond_ref[...]), w_ref[0]) + b_ref[0]


def _ada_mods(cond, ada_w, ada_b):
    depth, d, n = ada_w.shape
    nb = ADA_COLS_PER_STEP
    return pl.pallas_call(
        _ada_body,
        grid=(depth, n // nb),
        in_specs=[
            pl.BlockSpec((COND_ROWS, d), lambda l, j: (0, 0)),
            pl.BlockSpec((1, d, nb), lambda l, j: (l, 0, j)),
            pl.BlockSpec((1, 1, nb), lambda l, j: (l, 0, j)),
        ],
        out_specs=pl.BlockSpec((1, COND_ROWS, nb), lambda l, j: (l, 0, j)),
        out_shape=jax.ShapeDtypeStruct((depth, COND_ROWS, n), F32),
        name="ada_mods",
    )(cond, ada_w, ada_b.reshape(depth, 1, n))


def _ctx_body(ctx_ref, shift_ref, scale_ref, nw_ref, w_ref, lg_ref, cum_ref, keep_ref,
              sf_ref, sb_ref, hb_ref):
    d = ctx_ref.shape[-1]
    gw = HEADS_PER_GROUP * (d // HEADS)
    subtiles = list(range(ctx_ref.shape[1] // MASK_ROWS))
    normalised, v = set(), {}

    def fetch(direction, s, g):
        rows = _subtile_rows(s, MASK_ROWS)
        if s not in normalised:
            hc = _rmsnorm(ctx_ref[0, rows], nw_ref[...])
            hb_ref[rows] = (hc * (1.0 + scale_ref[...]) + shift_ref[...]).astype(BF16)
            normalised.add(s)
        cols = lambda block: slice(block * d + g * gw, block * d + (g + 1) * gw)
        if (s, g) not in v:
            v[s, g] = _dot(hb_ref[rows], w_ref[:, cols(2)]).astype(BF16)
            yield
        f_pre = _dot(hb_ref[rows], w_ref[:, cols(direction)])
        yield
        return f_pre, v[s, g], None

    streams = []
    for direction, st_ref in enumerate((sf_ref, sb_ref)):
        st = st_ref.at[0]
        st[...] = jnp.zeros(st.shape, F32)
        order = subtiles[::-1] if direction else subtiles
        streams.append(_hgrn_items(direction, order, lambda s, g, dr=direction: fetch(dr, s, g),
                                   lg_ref, (cum_ref.at[direction], keep_ref.at[direction]),
                                   st, None))
    _interleave(*streams)


def _ctx_states(ctx, shift, scale, norm_w, w_in, lb_logits, cum, keep):
    bsz, tc, d = ctx.shape
    dh = d // HEADS
    const2 = lambda b: (0, 0)
    state_spec = pl.BlockSpec((1, HEADS, dh, dh), lambda b: (b, 0, 0, 0))
    state_shape = jax.ShapeDtypeStruct((bsz, HEADS, dh, dh), F32)
    return pl.pallas_call(
        _ctx_body,
        grid=(bsz,),
        in_specs=[
            pl.BlockSpec((1, tc, d), lambda b: (b, 0, 0)),
            pl.BlockSpec((1, d), const2),
            pl.BlockSpec((1, d), const2),
            pl.BlockSpec((1, d), const2),
            pl.BlockSpec((d, 3 * d), const2),
            pl.BlockSpec(lb_logits.shape, lambda b: (0, 0, 0, 0)),
            pl.BlockSpec(cum.shape, lambda b: (0, 0, 0)),
            pl.BlockSpec(keep.shape, lambda b: (0, 0, 0)),
        ],
        out_specs=[state_spec, state_spec],
        out_shape=[state_shape, state_shape],
        scratch_shapes=[pltpu.VMEM((tc, d), BF16)],
        compiler_params=_COMPILER_PARAMS,
        name="ctx_states",
    )(ctx, shift, scale, norm_w, w_in, lb_logits, cum, keep)


def _normalised_rows(x, shift_ref, scale_ref, nw_ref):
    return (_rmsnorm(x, nw_ref[...]) * (1.0 + scale_ref[0]) + shift_ref[0]).astype(BF16)


def _load_initial_state(s0_ref, st_ref):
    @pl.when(pl.program_id(1) == 0)
    def _():
        st_ref[...] = s0_ref[0]


def _fwd_body(x_ref, shift_ref, scale_ref, nw_ref, w_ref, lg_ref, cum_ref, keep_ref, s0_ref,
              of_ref, fb_ref, v_ref, q_ref, zg_ref, st_ref, hb_ref):
    d = x_ref.shape[-1]
    dh = d // HEADS
    gw = HEADS_PER_GROUP * dh
    _load_initial_state(s0_ref, st_ref)
    normalised = set()

    def fetch(s, g):
        rows = _subtile_rows(s, FWD_SUBTILE)
        if s not in normalised:
            hb_ref[rows] = _normalised_rows(x_ref[0, rows], shift_ref, scale_ref, nw_ref)
            normalised.add(s)
        lanes = slice(g * gw, (g + 1) * gw)
        project = lambda block: _dot(
            hb_ref[rows], w_ref[:, block * d + g * gw:block * d + (g + 1) * gw])
        f_fwd = project(0)
        yield
        v = project(2).astype(BF16)
        v_ref[0, rows, lanes] = v
        yield
        q = _silu(project(3)) * (dh ** -0.5)
        q_ref[0, rows, lanes] = q.astype(BF16)
        yield
        fb_ref[0, rows, lanes] = project(1).astype(BF16)
        yield
        zg_ref[0, rows, lanes] = _silu(project(4)).astype(BF16)
        yield
        return f_fwd, v, q

    def emit(s, h, o):
        of_ref[0, _subtile_rows(s, FWD_SUBTILE), h * dh:(h + 1) * dh] = o.astype(BF16)

    subtiles = list(range(x_ref.shape[1] // FWD_SUBTILE))
    _interleave(_hgrn_items(0, subtiles, fetch, lg_ref, (cum_ref.at[0], keep_ref.at[0]), st_ref,
                            emit))


def _fwd_sweep(x, shift, scale, norm_w, w_in, lb_logits, cum, keep, s0_f):
    bsz, t, d = x.shape
    dh = d // HEADS
    tq = TOKENS_PER_STEP
    tile = pl.BlockSpec((1, tq, d), lambda b, i: (b, i, 0))
    row = pl.BlockSpec((1, 1, d), lambda b, i: (b, 0, 0))
    saved = jax.ShapeDtypeStruct((bsz, t, d), BF16)
    return pl.pallas_call(
        _fwd_body,
        grid=(bsz, t // tq),
        in_specs=[
            tile, row, row,
            pl.BlockSpec((1, d), lambda b, i: (0, 0)),
            _resident(w_in), _resident(lb_logits), _resident(cum), _resident(keep),
            pl.BlockSpec((1, HEADS, dh, dh), lambda b, i: (b, 0, 0, 0)),
        ],
        out_specs=[tile] * 5,
        out_shape=[saved] * 5,
        scratch_shapes=[pltpu.VMEM((HEADS, dh, dh), F32), pltpu.VMEM((tq, d), BF16)],
        compiler_params=_COMPILER_PARAMS,
        name="hgrn_fwd_sweep",
    )(x, shift, scale, norm_w, w_in, lb_logits, cum, keep, s0_f)


def _pool_stream(x1, shift_ref, scale_ref, gate_ref, nw_ref, wi_ref, band_ref, inv_ref, wg_ref,
                 ps_ref, wo_ref, fw_ref, hb_ref, act_ref, store):
    d = x1.shape[-1]
    gd = d // len(POOL_WINDOWS)
    hb_ref[...] = _normalised_rows(x1, shift_ref, scale_ref, nw_ref)

    def issue(g):
        u = _dot(hb_ref[...], wi_ref[:, g * gd:(g + 1) * gd])
        yield
        z = _dot(hb_ref[...], wi_ref[:, d + g * gd:d + (g + 1) * gd])
        yield
        return u, z

    def prepare(g, projected):
        u, z = projected
        return (u, z) + _split_bf16(u)

    def first(g, prepared):
        u, z, u_hi, u_lo = prepared
        band = band_ref[g]
        mean = _by_mask_rows(lambda hi, lo: (_dot(band, hi) + _dot(band, lo)) * inv_ref[g],
                             u_hi, u_lo)
        yield
        return g, u, z, mean

    def second(cont):
        g, u, z, mean = cont
        pooled = mean - u
        y = _dot(pooled.astype(BF16), wg_ref[g]) * ps_ref[:, g * gd:(g + 1) * gd]
        act_ref[:, g * gd:(g + 1) * gd] = (y * _silu(z)).astype(BF16)
        yield

    def finish_after(g):
        if g != len(POOL_WINDOWS) - 1:
            return None

        def finish():
            mix = _dot(act_ref[...], wo_ref[...])
            store(_rmsnorm(x1 + gate_ref[0] * mix, fw_ref[...]))

        return finish

    return _pipeline(list(range(len(POOL_WINDOWS))), issue, prepare, first, second, finish_after)


def _bwd_body(x_ref, of_ref, fb_ref, v_ref, q_ref, zg_ref, gate0_ref, lg_ref, cum_ref, keep_ref,
              s0_ref, gw_ref, wo_ref, shift1_ref, scale1_ref, gate1_ref, nw1_ref, wi_ref,
              band_ref, inv_ref, wg_ref, ps_ref, wo1_ref, fw_ref, out_ref, st_ref, act0_ref,
              hb1_ref, act1_ref):
    d = x_ref.shape[-1]
    dh = d // HEADS
    gw = HEADS_PER_GROUP * dh
    _load_initial_state(s0_ref, st_ref)
    masks = (cum_ref.at[1], keep_ref.at[1])

    def fetch(s, g):
        at = (0, _subtile_rows(s, BWD_SUBTILE), slice(g * gw, (g + 1) * gw))
        yield
        return fb_ref[at].astype(F32), v_ref[at], q_ref[at].astype(F32)

    def emit(s, h, o):
        at = (0, _subtile_rows(s, BWD_SUBTILE), slice(h * dh, (h + 1) * dh))
        o = _rmsnorm(of_ref[at].astype(F32) + o, gw_ref[:, at[2]])
        act0_ref[at[1:]] = (o * zg_ref[at].astype(F32)).astype(BF16)

    def all_layers(s):
        yield from _hgrn_items(1, [s], fetch, lg_ref, masks, st_ref, emit, order)
        rows = _subtile_rows(s, BWD_SUBTILE)
        x1 = x_ref[0, rows] + gate0_ref[0] * _dot(act0_ref[rows], wo_ref[...])
        yield

        def store(out):
            out_ref[0, rows] = out

        yield from _pool_stream(x1, shift1_ref, scale1_ref, gate1_ref, nw1_ref, wi_ref, band_ref,
                                inv_ref, wg_ref, ps_ref, wo1_ref, fw_ref, hb1_ref.at[rows],
                                act1_ref.at[rows], store)

    subtiles = list(range(x_ref.shape[1] // BWD_SUBTILE))[::-1]
    order = _StateOrder(subtiles)
    _interleave(*[all_layers(s) for s in subtiles], lag=STREAM_LAG)


def _bwd_sweep(x, saved, gate0, lb_logits, cum, keep, s0_b, gnorm_w, w_out, shift1, scale1, gate1,
               norm_w1, pool_w_in, band, inv_count, w_grp, pool_scale, pool_w_out, final_w):
    bsz, t, d = x.shape
    dh = d // HEADS
    tq = TOKENS_PER_STEP
    nt = t // tq
    tile = pl.BlockSpec((1, tq, d), lambda b, i: (b, nt - 1 - i, 0))
    row = pl.BlockSpec((1, 1, d), lambda b, i: (b, 0, 0))
    vec = pl.BlockSpec((1, d), lambda b, i: (0, 0))
    whole = _resident
    return pl.pallas_call(
        _bwd_body,
        grid=(bsz, nt),
        in_specs=[tile] * 6 + [
            row, whole(lb_logits), whole(cum), whole(keep),
            pl.BlockSpec((1, HEADS, dh, dh), lambda b, i: (b, 0, 0, 0)),
            vec, whole(w_out), row, row, row, vec, whole(pool_w_in), whole(band),
            whole(inv_count), whole(w_grp), vec, whole(pool_w_out), vec,
        ],
        out_specs=tile,
        out_shape=jax.ShapeDtypeStruct((bsz, t, d), F32),
        scratch_shapes=[pltpu.VMEM((HEADS, dh, dh), F32)] + [pltpu.VMEM((tq, d), BF16)] * 3,
        compiler_params=_COMPILER_PARAMS,
        name="hgrn_bwd_pool_sweep",
    )(x, *saved, gate0, lb_logits, cum, keep, s0_b, gnorm_w, w_out, shift1, scale1, gate1,
      norm_w1, pool_w_in, band, inv_count, w_grp, pool_scale, pool_w_out, final_w)


def _cumulative_matrices(n):
    r = jnp.arange(n)[:, None]
    c = jnp.arange(n)[None, :]
    same = (r // CHUNK) == (c // CHUNK)
    cum = jnp.stack([same & (c <= r), same & (c >= r)])
    return cum.astype(BF16), cum[:, :INTRA_ROWS, :INTRA_ROWS].astype(F32)


def _pool_windows(n):
    r = jnp.arange(n)[:, None]
    c = jnp.arange(n)[None, :]
    same = (r // GRID_W) == (c // GRID_W)
    band = jnp.stack([(same & (c >= r - w // 2) & (c < r - w // 2 + w)) for w in POOL_WINDOWS])
    count = jnp.sum(band.astype(F32), axis=-1, keepdims=True)
    return band.astype(BF16), 1.0 / count


def kernel(x, c, ctx, c_ctx, ada_w, ada_b, norm_w, hgrn_w_in, hgrn_lb_logits, hgrn_gnorm_w,
           hgrn_w_out, pool_w_in, pool_w_grp, pool_scale, pool_w_out, final_norm_w):
    bsz, t, d = x.shape
    depth = ada_w.shape[0]
    assert depth == 2 and hgrn_w_in.shape[0] == 1 and pool_w_in.shape[0] == 1
    assert t % TOKENS_PER_STEP == 0 and ctx.shape[1] % MASK_ROWS == 0 and bsz < COND_ROWS
    assert TOKENS_PER_STEP % FWD_SUBTILE == 0 and TOKENS_PER_STEP % BWD_SUBTILE == 0
    assert FWD_SUBTILE % MASK_ROWS == 0 and BWD_SUBTILE % MASK_ROWS == 0
    assert MASK_ROWS % INTRA_ROWS == 0 and INTRA_ROWS % CHUNK == 0 and MASK_ROWS % GRID_W == 0

    cond = jnp.zeros((COND_ROWS, d), F32).at[:bsz].set(c).at[bsz].set(c_ctx)
    mods = _ada_mods(cond, ada_w, ada_b)
    shift, scale, gate = (mods[:, :, i * d:(i + 1) * d] for i in range(3))
    per_batch = lambda m, l: m[l, :bsz].reshape(bsz, 1, d)

    w_in = hgrn_w_in[0].astype(BF16)
    lb_logits = hgrn_lb_logits.reshape(hgrn_lb_logits.shape[0], 2, 1, d)
    cum, keep = _cumulative_matrices(MASK_ROWS)
    band, inv_count = _pool_windows(MASK_ROWS)

    s0_f, s0_b = _ctx_states(ctx, shift[0, bsz:bsz + 1], scale[0, bsz:bsz + 1], norm_w[0:1],
                             w_in, lb_logits, cum, keep)
    saved = _fwd_sweep(x, per_batch(shift, 0), per_batch(scale, 0), norm_w[0:1], w_in,
                       lb_logits, cum, keep, s0_f)
    return _bwd_sweep(x, saved, per_batch(gate, 0), lb_logits, cum, keep, s0_b, hgrn_gnorm_w[0:1],
                      hgrn_w_out[0].astype(BF16), per_batch(shift, 1), per_batch(scale, 1),
                      per_batch(gate, 1), norm_w[1:2], pool_w_in[0].astype(BF16), band, inv_count,
                      pool_w_grp[0].astype(BF16), pool_scale[0:1], pool_w_out[0].astype(BF16),
                      final_norm_w.reshape(1, d))
```

```python
import jax
import jax.numpy as jnp
from jax import lax
from jax.experimental import pallas as pl
from jax.experimental.pallas import tpu as pltpu

HEADS = 8
CHUNK = 64
GRID_W = 64
POOL_WINDOWS = (2, 4, 8, 16)
EPS = 1e-6
MASK_ROWS = 256
FWD_SUBTILE = 512
BWD_SUBTILE = 256
TOKENS_PER_STEP = 512
HEADS_PER_GROUP = 2
INTRA_ROWS = 128
LOOKAHEAD = 2
STREAM_LAG = 12
ADA_COLS_PER_STEP = 512
COND_ROWS = 16
VMEM_LIMIT_BYTES = 56 * 1024 * 1024

F32 = jnp.float32
BF16 = jnp.bfloat16

_COMPILER_PARAMS = pltpu.CompilerParams(vmem_limit_bytes=VMEM_LIMIT_BYTES)


def _silu(a):
    half = 0.5 * a
    return half + half * jnp.tanh(half)


def _rmsnorm(xf, w):
    y = xf * lax.rsqrt(jnp.mean(xf * xf, axis=-1, keepdims=True) + EPS)
    return y * w


def _dot(a, b):
    return jnp.dot(a, b, preferred_element_type=F32)


def _dot_nt(a, b):
    return lax.dot_general(a, b, (((1,), (1,)), ((), ())), preferred_element_type=F32)


def _dot_tn(a, b):
    return lax.dot_general(a, b, (((0,), (0,)), ((), ())), preferred_element_type=F32)


def _lower_bound(logits, layer):
    rows = [logits[n] for n in range(logits.shape[0])]
    m = jnp.zeros_like(rows[0])
    for r in rows:
        m = jnp.maximum(m, r)
    exps = [jnp.exp(r - m) for r in rows]
    denom = jnp.exp(-m)
    for e in exps:
        denom = denom + e
    acc = exps[0]
    for e in exps[1:layer + 1]:
        acc = acc + e
    return acc / denom


def _subtile_rows(s, size):
    return slice(s * size, (s + 1) * size)


def _by_mask_rows(fn, *arrays):
    n = arrays[0].shape[0] // MASK_ROWS
    blocks = [fn(*(a[i * MASK_ROWS:(i + 1) * MASK_ROWS] for a in arrays)) for i in range(n)]
    return blocks[0] if n == 1 else jnp.concatenate(blocks, axis=0)


def _resident(a):
    return pl.BlockSpec(a.shape, lambda b, i: (0,) * a.ndim, pipeline_mode=pl.Buffered(1))


class _Steps:
    def __init__(self, gen):
        self._gen = gen
        self.done = False
        self.value = None

    def step(self):
        if not self.done:
            try:
                next(self._gen)
            except StopIteration as stop:
                self.done = True
                self.value = stop.value
        return not self.done


def _alongside(main, side):
    while main.step():
        yield
        if side.step():
            yield
    return main.value


def _pipeline(items, issue, prepare, first, second, finish_after):
    ahead = min(LOOKAHEAD, len(items))
    issued = []
    for item in items[:ahead]:
        issued.append((yield from issue(item)))
    prepared = []
    for item, handle in zip(items, issued):
        prepared.append(prepare(item, handle))
        yield
    deferred = []
    for i, item in enumerate(items):
        later = items[i + ahead] if i + ahead < len(items) else None
        issuing = _Steps(issue(later) if later is not None else iter(()))
        cont = yield from _alongside(_Steps(first(item, prepared[i])), issuing)
        for fn in deferred:
            fn()
            yield
        deferred = []
        yield from _alongside(_Steps(second(cont)), issuing)
        while issuing.step():
            yield
        fn = finish_after(item)
        if fn is not None:
            deferred.append(fn)
        if later is not None:
            prepared.append(prepare(later, issuing.value))
            yield
    for fn in deferred:
        fn()
        yield


def _interleave(*streams, lag=0):
    done = [False] * len(streams)
    rnd = 0
    while not all(done):
        for k, stream in enumerate(streams):
            if done[k] or rnd < k * lag:
                continue
            try:
                next(stream)
            except StopIteration:
                done[k] = True
        rnd += 1


def _scan_head(reverse, b, k, v, q, keep, st):
    tq = b.shape[0]
    nc = tq // CHUNK
    per_block = INTRA_ROWS // CHUNK
    mid = CHUNK // 2 if reverse else CHUNK // 2 - 1
    last = 0 if reverse else CHUNK - 1
    k_dec, q_dec, k_end, q_in, decay = [], [], [], [], []
    for n in range(nc):
        rows = slice(n * CHUNK, (n + 1) * CHUNK)
        bn = b[rows]
        b_mid = bn[mid:mid + 1]
        b_last = bn[last:last + 1]
        kd = k[rows] * jnp.exp2(b_mid - bn)
        k_dec.append(kd.astype(BF16))
        k_end.append((kd * jnp.exp2(b_last - b_mid)).astype(BF16))
        decay.append(jnp.exp2(b_last))
        if q is not None:
            qd = q[rows] * jnp.exp2(bn - b_mid)
            q_dec.append(qd.astype(BF16))
            q_in.append((qd * jnp.exp2(b_mid)).astype(BF16))
    scores = []
    if q is not None:
        for j in range(nc // per_block):
            blk = slice(j * per_block, (j + 1) * per_block)
            scores.append(_dot_nt(jnp.concatenate(q_dec[blk], axis=0),
                                  jnp.concatenate(k_dec[blk], axis=0)))
        yield
    update = [_dot_tn(v[n * CHUNK:(n + 1) * CHUNK], k_end[n]) for n in range(nc)]
    yield
    if q is not None:
        scores = [jnp.where(keep, a, 0.0).astype(BF16) for a in scores]
        o_intra = [_dot(a, v[j * INTRA_ROWS:(j + 1) * INTRA_ROWS]) for j, a in enumerate(scores)]
        yield
    carried = [None] * nc
    for n in (range(nc - 1, -1, -1) if reverse else range(nc)):
        carried[n] = st.astype(BF16)
        st = st * decay[n] + update[n]
    if q is None:
        return None, st
    o_inter = [_dot_nt(q_in[n], carried[n]) for n in range(nc)]
    return jnp.concatenate(o_intra, axis=0) + jnp.concatenate(o_inter, axis=0), st


class _StateOrder:
    def __init__(self, subtiles):
        self._before = {s: subtiles[:i] for i, s in enumerate(subtiles)}
        self._written = set()

    def read(self, s, h):
        missing = [p for p in self._before[s] if (p, h) not in self._written]
        assert not missing, f"state of head {h} read for subtile {s} before {missing} wrote it"

    def wrote(self, s, h):
        self._written.add((s, h))


def _hgrn_items(direction, subtiles, fetch, lg_ref, masks, st_ref, emit, order=None):
    dh = st_ref.shape[-1]
    gw = HEADS_PER_GROUP * dh
    reverse = direction == 1
    cum_ref, keep_ref = masks
    cum = cum_ref[...]
    keep = keep_ref[...] != 0.0
    lb_all = _lower_bound(lg_ref[:, direction], 0)

    def issue(item):
        return (yield from fetch(*item))

    def prepare(item, fetched):
        s, g = item
        f_pre, v, q = fetched
        lb = lb_all[:, g * gw:(g + 1) * gw]
        f = 0.5 * (1.0 + lb) + (0.5 * (1.0 - lb)) * jnp.tanh(0.5 * f_pre)
        b = _by_mask_rows(lambda logf: _dot(cum, logf), jnp.log2(f).astype(BF16))
        return b, 1.0 - f, v, q

    def first(item, prepared):
        s, g = item
        b, k, v, q = prepared
        heads = []
        for hh in range(HEADS_PER_GROUP):
            lanes = slice(hh * dh, (hh + 1) * dh)
            if order is not None:
                order.read(s, g * HEADS_PER_GROUP + hh)
            head = _Steps(_scan_head(reverse, b[:, lanes], k[:, lanes], v[:, lanes],
                                     None if q is None else q[:, lanes], keep,
                                     st_ref[g * HEADS_PER_GROUP + hh]))
            for _ in range(1 if q is None else 2):
                head.step()
                yield
            heads.append(head)
        return item, heads

    def second(cont):
        (s, g), heads = cont
        for hh, head in enumerate(heads):
            h = g * HEADS_PER_GROUP + hh
            while head.step():
                yield
            o, st = head.value
            st_ref[h] = st
            if order is not None:
                order.wrote(s, h)
            if emit is not None:
                emit(s, h, o)
            yield

    items = [(s, g) for s in subtiles for g in range(HEADS // HEADS_PER_GROUP)]
    return _pipeline(items, issue, prepare, first, second, lambda item: None)


def _ada_body(cond_ref, w_ref, b_ref, o_ref):
    o_ref[0] = _dot(_silu(cond_ref[...]), w_ref[0]) + b_ref[0]


def _ada_mods(cond, ada_w, ada_b):
    depth, d, n = ada_w.shape
    nb = ADA_COLS_PER_STEP
    return pl.pallas_call(
        _ada_body,
        grid=(depth, n // nb),
        in_specs=[
            pl.BlockSpec((COND_ROWS, d), lambda l, j: (0, 0)),
            pl.BlockSpec((1, d, nb), lambda l, j: (l, 0, j)),
            pl.BlockSpec((1, 1, nb), lambda l, j: (l, 0, j)),
        ],
        out_specs=pl.BlockSpec((1, COND_ROWS, nb), lambda l, j: (l, 0, j)),
        out_shape=jax.ShapeDtypeStruct((depth, COND_ROWS, n), F32),
        name="ada_mods",
    )(cond, ada_w, ada_b.reshape(depth, 1, n))


def _ctx_body(ctx_ref, shift_ref, scale_ref, nw_ref, w_ref, lg_ref, cum_ref, keep_ref,
              sf_ref, sb_ref, hb_ref):
    d = ctx_ref.shape[-1]
    gw = HEADS_PER_GROUP * (d // HEADS)
    subtiles = list(range(ctx_ref.shape[1] // MASK_ROWS))
    normalised, v = set(), {}

    def fetch(direction, s, g):
        rows = _subtile_rows(s, MASK_ROWS)
        if s not in normalised:
            hc = _rmsnorm(ctx_ref[0, rows], nw_ref[...])
            hb_ref[rows] = (hc * (1.0 + scale_ref[...]) + shift_ref[...]).astype(BF16)
            normalised.add(s)
        cols = lambda block: slice(block * d + g * gw, block * d + (g + 1) * gw)
        if (s, g) not in v:
            v[s, g] = _dot(hb_ref[rows], w_ref[:, cols(2)]).astype(BF16)
            yield
        f_pre = _dot(hb_ref[rows], w_ref[:, cols(direction)])
        yield
        return f_pre, v[s, g], None

    streams = []
    for direction, st_ref in enumerate((sf_ref, sb_ref)):
        st = st_ref.at[0]
        st[...] = jnp.zeros(st.shape, F32)
        order = subtiles[::-1] if direction else subtiles
        streams.append(_hgrn_items(direction, order, lambda s, g, dr=direction: fetch(dr, s, g),
                                   lg_ref, (cum_ref.at[direction], keep_ref.at[direction]),
                                   st, None))
    _interleave(*streams)


def _ctx_states(ctx, shift, scale, norm_w, w_in, lb_logits, cum, keep):
    bsz, tc, d = ctx.shape
    dh = d // HEADS
    const2 = lambda b: (0, 0)
    state_spec = pl.BlockSpec((1, HEADS, dh, dh), lambda b: (b, 0, 0, 0))
    state_shape = jax.ShapeDtypeStruct((bsz, HEADS, dh, dh), F32)
    return pl.pallas_call(
        _ctx_body,
        grid=(bsz,),
        in_specs=[
            pl.BlockSpec((1, tc, d), lambda b: (b, 0, 0)),
            pl.BlockSpec((1, d), const2),
            pl.BlockSpec((1, d), const2),
            pl.BlockSpec((1, d), const2),
            pl.BlockSpec((d, 3 * d), const2),
            pl.BlockSpec(lb_logits.shape, lambda b: (0, 0, 0, 0)),
            pl.BlockSpec(cum.shape, lambda b: (0, 0, 0)),
            pl.BlockSpec(keep.shape, lambda b: (0, 0, 0)),
        ],
        out_specs=[state_spec, state_spec],
        out_shape=[state_shape, state_shape],
        scratch_shapes=[pltpu.VMEM((tc, d), BF16)],
        compiler_params=_COMPILER_PARAMS,
        name="ctx_states",
    )(ctx, shift, scale, norm_w, w_in, lb_logits, cum, keep)


def _normalised_rows(x, shift_ref, scale_ref, nw_ref):
    return (_rmsnorm(x, nw_ref[...]) * (1.0 + scale_ref[0]) + shift_ref[0]).astype(BF16)


def _load_initial_state(s0_ref, st_ref):
    @pl.when(pl.program_id(1) == 0)
    def _():
        st_ref[...] = s0_ref[0]


def _fwd_body(x_ref, shift_ref, scale_ref, nw_ref, w_ref, lg_ref, cum_ref, keep_ref, s0_ref,
              of_ref, fb_ref, v_ref, q_ref, zg_ref, st_ref, hb_ref):
    d = x_ref.shape[-1]
    dh = d // HEADS
    gw = HEADS_PER_GROUP * dh
    _load_initial_state(s0_ref, st_ref)
    normalised = set()

    def fetch(s, g):
        rows = _subtile_rows(s, FWD_SUBTILE)
        if s not in normalised:
            hb_ref[rows] = _normalised_rows(x_ref[0, rows], shift_ref, scale_ref, nw_ref)
            normalised.add(s)
        lanes = slice(g * gw, (g + 1) * gw)
        project = lambda block: _dot(
            hb_ref[rows], w_ref[:, block * d + g * gw:block * d + (g + 1) * gw])
        f_fwd = project(0)
        yield
        v = project(2).astype(BF16)
        v_ref[0, rows, lanes] = v
        yield
        q = _silu(project(3)) * (dh ** -0.5)
        q_ref[0, rows, lanes] = q.astype(BF16)
        yield
        fb_ref[0, rows, lanes] = project(1).astype(BF16)
        yield
        zg_ref[0, rows, lanes] = _silu(project(4)).astype(BF16)
        yield
        return f_fwd, v, q

    def emit(s, h, o):
        of_ref[0, _subtile_rows(s, FWD_SUBTILE), h * dh:(h + 1) * dh] = o.astype(BF16)

    subtiles = list(range(x_ref.shape[1] // FWD_SUBTILE))
    _interleave(_hgrn_items(0, subtiles, fetch, lg_ref, (cum_ref.at[0], keep_ref.at[0]), st_ref,
                            emit))


def _fwd_sweep(x, shift, scale, norm_w, w_in, lb_logits, cum, keep, s0_f):
    bsz, t, d = x.shape
    dh = d // HEADS
    tq = TOKENS_PER_STEP
    tile = pl.BlockSpec((1, tq, d), lambda b, i: (b, i, 0))
    row = pl.BlockSpec((1, 1, d), lambda b, i: (b, 0, 0))
    saved = jax.ShapeDtypeStruct((bsz, t, d), BF16)
    return pl.pallas_call(
        _fwd_body,
        grid=(bsz, t // tq),
        in_specs=[
            tile, row, row,
            pl.BlockSpec((1, d), lambda b, i: (0, 0)),
            _resident(w_in), _resident(lb_logits), _resident(cum), _resident(keep),
            pl.BlockSpec((1, HEADS, dh, dh), lambda b, i: (b, 0, 0, 0)),
        ],
        out_specs=[tile] * 5,
        out_shape=[saved] * 5,
        scratch_shapes=[pltpu.VMEM((HEADS, dh, dh), F32), pltpu.VMEM((tq, d), BF16)],
        compiler_params=_COMPILER_PARAMS,
        name="hgrn_fwd_sweep",
    )(x, shift, scale, norm_w, w_in, lb_logits, cum, keep, s0_f)


def _pool_stream(x1, shift_ref, scale_ref, gate_ref, nw_ref, wi_ref, band_ref, inv_ref, wg_ref,
                 ps_ref, wo_ref, fw_ref, hb_ref, act_ref, store):
    d = x1.shape[-1]
    gd = d // len(POOL_WINDOWS)
    hb_ref[...] = _normalised_rows(x1, shift_ref, scale_ref, nw_ref)

    def issue(g):
        u = _dot(hb_ref[...], wi_ref[:, g * gd:(g + 1) * gd])
        yield
        z = _dot(hb_ref[...], wi_ref[:, d + g * gd:d + (g + 1) * gd])
        yield
        return u, z

    def prepare(g, projected):
        u, z = projected
        return u, z, u.astype(BF16)

    def first(g, prepared):
        u, z, u_bf = prepared
        band = band_ref[g]
        mean = _by_mask_rows(lambda rows: _dot(band, rows) * inv_ref[g], u_bf)
        yield
        return g, u, z, mean

    def second(cont):
        g, u, z, mean = cont
        pooled = mean - u
        y = _dot(pooled.astype(BF16), wg_ref[g]) * ps_ref[:, g * gd:(g + 1) * gd]
        act_ref[:, g * gd:(g + 1) * gd] = (y * _silu(z)).astype(BF16)
        yield

    def finish_after(g):
        if g != len(POOL_WINDOWS) - 1:
            return None

        def finish():
            mix = _dot(act_ref[...], wo_ref[...])
            store(_rmsnorm(x1 + gate_ref[0] * mix, fw_ref[...]))

        return finish

    return _pipeline(list(range(len(POOL_WINDOWS))), issue, prepare, first, second, finish_after)


def _bwd_body(x_ref, of_ref, fb_ref, v_ref, q_ref, zg_ref, gate0_ref, lg_ref, cum_ref, keep_ref,
              s0_ref, gw_ref, wo_ref, shift1_ref, scale1_ref, gate1_ref, nw1_ref, wi_ref,
              band_ref, inv_ref, wg_ref, ps_ref, wo1_ref, fw_ref, out_ref, st_ref, act0_ref,
              hb1_ref, act1_ref):
    d = x_ref.shape[-1]
    dh = d // HEADS
    gw = HEADS_PER_GROUP * dh
    _load_initial_state(s0_ref, st_ref)
    masks = (cum_ref.at[1], keep_ref.at[1])

    def fetch(s, g):
        at = (0, _subtile_rows(s, BWD_SUBTILE), slice(g * gw, (g + 1) * gw))
        yield
        return fb_ref[at].astype(F32), v_ref[at], q_ref[at].astype(F32)

    def emit(s, h, o):
        at = (0, _subtile_rows(s, BWD_SUBTILE), slice(h * dh, (h + 1) * dh))
        o = _rmsnorm(of_ref[at].astype(F32) + o, gw_ref[:, at[2]])
        act0_ref[at[1:]] = (o * zg_ref[at].astype(F32)).astype(BF16)

    def all_layers(s):
        yield from _hgrn_items(1, [s], fetch, lg_ref, masks, st_ref, emit, order)
        rows = _subtile_rows(s, BWD_SUBTILE)
        x1 = x_ref[0, rows] + gate0_ref[0] * _dot(act0_ref[rows], wo_ref[...])
        yield

        def store(out):
            out_ref[0, rows] = out

        yield from _pool_stream(x1, shift1_ref, scale1_ref, gate1_ref, nw1_ref, wi_ref, band_ref,
                                inv_ref, wg_ref, ps_ref, wo1_ref, fw_ref, hb1_ref.at[rows],
                                act1_ref.at[rows], store)

    subtiles = list(range(x_ref.shape[1] // BWD_SUBTILE))[::-1]
    order = _StateOrder(subtiles)
    _interleave(*[all_layers(s) for s in subtiles], lag=STREAM_LAG)


def _bwd_sweep(x, saved, gate0, lb_logits, cum, keep, s0_b, gnorm_w, w_out, shift1, scale1, gate1,
               norm_w1, pool_w_in, band, inv_count, w_grp, pool_scale, pool_w_out, final_w):
    bsz, t, d = x.shape
    dh = d // HEADS
    tq = TOKENS_PER_STEP
    nt = t // tq
    tile = pl.BlockSpec((1, tq, d), lambda b, i: (b, nt - 1 - i, 0))
    row = pl.BlockSpec((1, 1, d), lambda b, i: (b, 0, 0))
    vec = pl.BlockSpec((1, d), lambda b, i: (0, 0))
    whole = _resident
    return pl.pallas_call(
        _bwd_body,
        grid=(bsz, nt),
        in_specs=[tile] * 6 + [
            row, whole(lb_logits), whole(cum), whole(keep),
            pl.BlockSpec((1, HEADS, dh, dh), lambda b, i: (b, 0, 0, 0)),
            vec, whole(w_out), row, row, row, vec, whole(pool_w_in), whole(band),
            whole(inv_count), whole(w_grp), vec, whole(pool_w_out), vec,
        ],
        out_specs=tile,
        out_shape=jax.ShapeDtypeStruct((bsz, t, d), F32),
        scratch_shapes=[pltpu.VMEM((HEADS, dh, dh), F32)] + [pltpu.VMEM((tq, d), BF16)] * 3,
        compiler_params=_COMPILER_PARAMS,
        name="hgrn_bwd_pool_sweep",
    )(x, *saved, gate0, lb_logits, cum, keep, s0_b, gnorm_w, w_out, shift1, scale1, gate1,
      norm_w1, pool_w_in, band, inv_count, w_grp, pool_scale, pool_w_out, final_w)


def _cumulative_matrices(n):
    r = jnp.arange(n)[:, None]
    c = jnp.arange(n)[None, :]
    same = (r // CHUNK) == (c // CHUNK)
    cum = jnp.stack([same & (c <= r), same & (c >= r)])
    return cum.astype(BF16), cum[:, :INTRA_ROWS, :INTRA_ROWS].astype(F32)


def _pool_windows(n):
    r = jnp.arange(n)[:, None]
    c = jnp.arange(n)[None, :]
    same = (r // GRID_W) == (c // GRID_W)
    band = jnp.stack([(same & (c >= r - w // 2) & (c < r - w // 2 + w)) for w in POOL_WINDOWS])
    count = jnp.sum(band.astype(F32), axis=-1, keepdims=True)
    return band.astype(BF16), 1.0 / count


def kernel(x, c, ctx, c_ctx, ada_w, ada_b, norm_w, hgrn_w_in, hgrn_lb_logits, hgrn_gnorm_w,
           hgrn_w_out, pool_w_in, pool_w_grp, pool_scale, pool_w_out, final_norm_w):
    bsz, t, d = x.shape
    depth = ada_w.shape[0]
    assert depth == 2 and hgrn_w_in.shape[0] == 1 and pool_w_in.shape[0] == 1
    assert t % TOKENS_PER_STEP == 0 and ctx.shape[1] % MASK_ROWS == 0 and bsz < COND_ROWS
    assert TOKENS_PER_STEP % FWD_SUBTILE == 0 and TOKENS_PER_STEP % BWD_SUBTILE == 0
    assert FWD_SUBTILE % MASK_ROWS == 0 and BWD_SUBTILE % MASK_ROWS == 0
    assert MASK_ROWS % INTRA_ROWS == 0 and INTRA_ROWS % CHUNK == 0 and MASK_ROWS % GRID_W == 0

    cond = jnp.zeros((COND_ROWS, d), F32).at[:bsz].set(c).at[bsz].set(c_ctx)
    mods = _ada_mods(cond, ada_w, ada_b)
    shift, scale, gate = (mods[:, :, i * d:(i + 1) * d] for i in range(3))
    per_batch = lambda m, l: m[l, :bsz].reshape(bsz, 1, d)

    w_in = hgrn_w_in[0].astype(BF16)
    lb_logits = hgrn_lb_logits.reshape(hgrn_lb_logits.shape[0], 2, 1, d)
    cum, keep = _cumulative_matrices(MASK_ROWS)
    band, inv_count = _pool_windows(MASK_ROWS)

    s0_f, s0_b = _ctx_states(ctx, shift[0, bsz:bsz + 1], scale[0, bsz:bsz + 1], norm_w[0:1],
                             w_in, lb_logits, cum, keep)
    saved = _fwd_sweep(x, per_batch(shift, 0), per_batch(scale, 0), norm_w[0:1], w_in,
                       lb_logits, cum, keep, s0_f)
    return _bwd_sweep(x, saved, per_batch(gate, 0), lb_logits, cum, keep, s0_b, hgrn_gnorm_w[0:1],
                      hgrn_w_out[0].astype(BF16), per_batch(shift, 1), per_batch(scale, 1),
                      per_batch(gate, 1), norm_w[1:2], pool_w_in[0].astype(BF16), band, inv_count,
                      pool_w_grp[0].astype(BF16), pool_scale[0:1], pool_w_out[0].astype(BF16),
                      final_norm_w.reshape(1, d))
```

```python
import jax
import jax.numpy as jnp
from jax import lax
from jax.experimental import pallas as pl
from jax.experimental.pallas import tpu as pltpu

HEADS = 8
CHUNK = 64
GRID_W = 64
POOL_WINDOWS = (2, 4, 8, 16)
EPS = 1e-6
MASK_ROWS = 256
FWD_SUBTILE = 512
BWD_SUBTILE = 256
FWD_TOKENS_PER_STEP = 1024
BWD_TOKENS_PER_STEP = 512
HEADS_PER_GROUP = 2
INTRA_ROWS = 128
LOOKAHEAD = 2
STREAM_LAG = 12
ADA_COLS_PER_STEP = 1536
CTX_BATCH_PER_STEP = 4
COND_ROWS = 16
VMEM_LIMIT_BYTES = 56 * 1024 * 1024

F32 = jnp.float32
BF16 = jnp.bfloat16

_COMPILER_PARAMS = pltpu.CompilerParams(vmem_limit_bytes=VMEM_LIMIT_BYTES)


def _silu(a):
    half = 0.5 * a
    return half + half * jnp.tanh(half)


def _rmsnorm(xf, w):
    y = xf * lax.rsqrt(jnp.mean(xf * xf, axis=-1, keepdims=True) + EPS)
    return y * w


def _dot(a, b):
    return jnp.dot(a, b, preferred_element_type=F32)


def _dot_nt(a, b):
    return lax.dot_general(a, b, (((1,), (1,)), ((), ())), preferred_element_type=F32)


def _dot_tn(a, b):
    return lax.dot_general(a, b, (((0,), (0,)), ((), ())), preferred_element_type=F32)


def _lower_bound(logits, layer):
    rows = [logits[n] for n in range(logits.shape[0])]
    m = jnp.zeros_like(rows[0])
    for r in rows:
        m = jnp.maximum(m, r)
    exps = [jnp.exp(r - m) for r in rows]
    denom = jnp.exp(-m)
    for e in exps:
        denom = denom + e
    acc = exps[0]
    for e in exps[1:layer + 1]:
        acc = acc + e
    return acc / denom


def _subtile_rows(s, size):
    return slice(s * size, (s + 1) * size)


def _by_mask_rows(fn, *arrays):
    n = arrays[0].shape[0] // MASK_ROWS
    blocks = [fn(*(a[i * MASK_ROWS:(i + 1) * MASK_ROWS] for a in arrays)) for i in range(n)]
    return blocks[0] if n == 1 else jnp.concatenate(blocks, axis=0)


def _resident(a):
    return pl.BlockSpec(a.shape, lambda b, i: (0,) * a.ndim, pipeline_mode=pl.Buffered(1))


class _Steps:
    def __init__(self, gen):
        self._gen = gen
        self.done = False
        self.value = None

    def step(self):
        if not self.done:
            try:
                next(self._gen)
            except StopIteration as stop:
                self.done = True
                self.value = stop.value
        return not self.done


def _alongside(main, side):
    while main.step():
        yield
        if side.step():
            yield
    return main.value


def _pipeline(items, issue, prepare, first, second, finish_after):
    ahead = min(LOOKAHEAD, len(items))
    issued = []
    for item in items[:ahead]:
        issued.append((yield from issue(item)))
    prepared = []
    for item, handle in zip(items, issued):
        prepared.append(prepare(item, handle))
        yield
    deferred = []
    for i, item in enumerate(items):
        later = items[i + ahead] if i + ahead < len(items) else None
        issuing = _Steps(issue(later) if later is not None else iter(()))
        cont = yield from _alongside(_Steps(first(item, prepared[i])), issuing)
        for fn in deferred:
            fn()
            yield
        deferred = []
        yield from _alongside(_Steps(second(cont)), issuing)
        while issuing.step():
            yield
        fn = finish_after(item)
        if fn is not None:
            deferred.append(fn)
        if later is not None:
            prepared.append(prepare(later, issuing.value))
            yield
    for fn in deferred:
        fn()
        yield


def _interleave(*streams, lag=0):
    done = [False] * len(streams)
    rnd = 0
    while not all(done):
        for k, stream in enumerate(streams):
            if done[k] or rnd < k * lag:
                continue
            try:
                next(stream)
            except StopIteration:
                done[k] = True
        rnd += 1


def _scan_head(reverse, b, k, v, q, keep, st):
    tq = b.shape[0]
    nc = tq // CHUNK
    per_block = INTRA_ROWS // CHUNK
    mid = CHUNK // 2 if reverse else CHUNK // 2 - 1
    last = 0 if reverse else CHUNK - 1
    k_dec, q_dec, k_end, q_in, decay = [], [], [], [], []
    for n in range(nc):
        rows = slice(n * CHUNK, (n + 1) * CHUNK)
        bn = b[rows]
        b_mid = bn[mid:mid + 1]
        b_last = bn[last:last + 1]
        kd = k[rows] * jnp.exp2(b_mid - bn)
        k_dec.append(kd.astype(BF16))
        k_end.append((kd * jnp.exp2(b_last - b_mid)).astype(BF16))
        decay.append(jnp.exp2(b_last))
        if q is not None:
            qd = q[rows] * jnp.exp2(bn - b_mid)
            q_dec.append(qd.astype(BF16))
            q_in.append((qd * jnp.exp2(b_mid)).astype(BF16))
    scores = []
    if q is not None:
        for j in range(nc // per_block):
            blk = slice(j * per_block, (j + 1) * per_block)
            scores.append(_dot_nt(jnp.concatenate(q_dec[blk], axis=0),
                                  jnp.concatenate(k_dec[blk], axis=0)))
        yield
    update = [_dot_tn(v[n * CHUNK:(n + 1) * CHUNK], k_end[n]) for n in range(nc)]
    yield
    if q is not None:
        scores = [jnp.where(keep, a, 0.0).astype(BF16) for a in scores]
        o_intra = [_dot(a, v[j * INTRA_ROWS:(j + 1) * INTRA_ROWS]) for j, a in enumerate(scores)]
        yield
    carried = [None] * nc
    for n in (range(nc - 1, -1, -1) if reverse else range(nc)):
        carried[n] = st.astype(BF16)
        st = st * decay[n] + update[n]
    if q is None:
        return None, st
    o_inter = [_dot_nt(q_in[n], carried[n]) for n in range(nc)]
    return jnp.concatenate(o_intra, axis=0) + jnp.concatenate(o_inter, axis=0), st


class _StateOrder:
    def __init__(self, subtiles):
        self._before = {s: subtiles[:i] for i, s in enumerate(subtiles)}
        self._written = set()

    def read(self, s, h):
        missing = [p for p in self._before[s] if (p, h) not in self._written]
        assert not missing, f"state of head {h} read for subtile {s} before {missing} wrote it"

    def wrote(self, s, h):
        self._written.add((s, h))


def _hgrn_items(direction, subtiles, fetch, lg_ref, masks, st_ref, emit, order=None):
    dh = st_ref.shape[-1]
    gw = HEADS_PER_GROUP * dh
    reverse = direction == 1
    cum_ref, keep_ref = masks
    cum = cum_ref[...]
    keep = keep_ref[...] != 0.0
    lb_all = _lower_bound(lg_ref[:, direction], 0)

    def issue(item):
        return (yield from fetch(*item))

    def prepare(item, fetched):
        s, g = item
        f_pre, v, q = fetched
        lb = lb_all[:, g * gw:(g + 1) * gw]
        f = 0.5 * (1.0 + lb) + (0.5 * (1.0 - lb)) * jnp.tanh(0.5 * f_pre)
        b = _by_mask_rows(lambda logf: _dot(cum, logf), jnp.log2(f).astype(BF16))
        return b, 1.0 - f, v, q

    def first(item, prepared):
        s, g = item
        b, k, v, q = prepared
        heads = []
        for hh in range(HEADS_PER_GROUP):
            lanes = slice(hh * dh, (hh + 1) * dh)
            if order is not None:
                order.read(s, g * HEADS_PER_GROUP + hh)
            head = _Steps(_scan_head(reverse, b[:, lanes], k[:, lanes], v[:, lanes],
                                     None if q is None else q[:, lanes], keep,
                                     st_ref[g * HEADS_PER_GROUP + hh]))
            for _ in range(1 if q is None else 2):
                head.step()
                yield
            heads.append(head)
        return item, heads

    def second(cont):
        (s, g), heads = cont
        for hh, head in enumerate(heads):
            h = g * HEADS_PER_GROUP + hh
            while head.step():
                yield
            o, st = head.value
            st_ref[h] = st
            if order is not None:
                order.wrote(s, h)
            if emit is not None:
                emit(s, h, o)
            yield

    items = [(s, g) for s in subtiles for g in range(HEADS // HEADS_PER_GROUP)]
    return _pipeline(items, issue, prepare, first, second, lambda item: None)


def _ada_body(cond_ref, w_ref, b_ref, o_ref):
    o_ref[0] = _dot(_silu(cond_ref[...]), w_ref[0]) + b_ref[0]


def _ada_mods(cond, ada_w, ada_b):
    depth, d, n = ada_w.shape
    nb = ADA_COLS_PER_STEP
    return pl.pallas_call(
        _ada_body,
        grid=(depth, n // nb),
        in_specs=[
            pl.BlockSpec((COND_ROWS, d), lambda l, j: (0, 0)),
            pl.BlockSpec((1, d, nb), lambda l, j: (l, 0, j)),
            pl.BlockSpec((1, 1, nb), lambda l, j: (l, 0, j)),
        ],
        out_specs=pl.BlockSpec((1, COND_ROWS, nb), lambda l, j: (l, 0, j)),
        out_shape=jax.ShapeDtypeStruct((depth, COND_ROWS, n), F32),
        compiler_params=_COMPILER_PARAMS,
        name="ada_mods",
    )(cond, ada_w, ada_b.reshape(depth, 1, n))


def _ctx_body(ctx_ref, shift_ref, scale_ref, nw_ref, w_ref, lg_ref, cum_ref, keep_ref,
              sf_ref, sb_ref, hb_ref):
    n_batch, tc, d = ctx_ref.shape
    gw = HEADS_PER_GROUP * (d // HEADS)
    subtiles = list(range(tc // MASK_ROWS))
    normalised, v = set(), {}

    def fetch(direction, bi, s, g):
        rows = _subtile_rows(s, MASK_ROWS)
        hb = hb_ref.at[bi]
        if (bi, s) not in normalised:
            hc = _rmsnorm(ctx_ref[bi, rows], nw_ref[...])
            hb[rows] = (hc * (1.0 + scale_ref[...]) + shift_ref[...]).astype(BF16)
            normalised.add((bi, s))
        cols = lambda block: slice(block * d + g * gw, block * d + (g + 1) * gw)
        if (bi, s, g) not in v:
            v[bi, s, g] = _dot(hb[rows], w_ref[:, cols(2)]).astype(BF16)
            yield
        f_pre = _dot(hb[rows], w_ref[:, cols(direction)])
        yield
        return f_pre, v[bi, s, g], None

    streams = []
    for bi in range(n_batch):
        for direction, st_ref in enumerate((sf_ref, sb_ref)):
            st = st_ref.at[bi]
            st[...] = jnp.zeros(st.shape, F32)
            order = subtiles[::-1] if direction else subtiles
            streams.append(_hgrn_items(
                direction, order, lambda s, g, dr=direction, bi=bi: fetch(dr, bi, s, g), lg_ref,
                (cum_ref.at[direction], keep_ref.at[direction]), st, None))
    _interleave(*streams)


def _ctx_states(ctx, shift, scale, norm_w, w_in, lb_logits, cum, keep):
    bsz, tc, d = ctx.shape
    dh = d // HEADS
    const2 = lambda b: (0, 0)
    nb = CTX_BATCH_PER_STEP
    state_spec = pl.BlockSpec((nb, HEADS, dh, dh), lambda b: (b, 0, 0, 0))
    state_shape = jax.ShapeDtypeStruct((bsz, HEADS, dh, dh), F32)
    return pl.pallas_call(
        _ctx_body,
        grid=(bsz // nb,),
        in_specs=[
            pl.BlockSpec((nb, tc, d), lambda b: (b, 0, 0)),
            pl.BlockSpec((1, d), const2),
            pl.BlockSpec((1, d), const2),
            pl.BlockSpec((1, d), const2),
            pl.BlockSpec((d, 3 * d), const2),
            pl.BlockSpec(lb_logits.shape, lambda b: (0, 0, 0, 0)),
            pl.BlockSpec(cum.shape, lambda b: (0, 0, 0)),
            pl.BlockSpec(keep.shape, lambda b: (0, 0, 0)),
        ],
        out_specs=[state_spec, state_spec],
        out_shape=[state_shape, state_shape],
        scratch_shapes=[pltpu.VMEM((nb, tc, d), BF16)],
        compiler_params=_COMPILER_PARAMS,
        name="ctx_states",
    )(ctx, shift, scale, norm_w, w_in, lb_logits, cum, keep)


def _normalised_rows(x, shift_ref, scale_ref, nw_ref):
    return (_rmsnorm(x, nw_ref[...]) * (1.0 + scale_ref[0]) + shift_ref[0]).astype(BF16)


def _load_initial_state(s0_ref, st_ref):
    @pl.when(pl.program_id(1) == 0)
    def _():
        st_ref[...] = s0_ref[0]


def _fwd_body(x_ref, shift_ref, scale_ref, nw_ref, w_ref, lg_ref, cum_ref, keep_ref, s0_ref,
              of_ref, fb_ref, v_ref, q_ref, zg_ref, st_ref, hb_ref):
    d = x_ref.shape[-1]
    dh = d // HEADS
    gw = HEADS_PER_GROUP * dh
    _load_initial_state(s0_ref, st_ref)
    normalised = set()

    def fetch(s, g):
        rows = _subtile_rows(s, FWD_SUBTILE)
        if s not in normalised:
            hb_ref[rows] = _normalised_rows(x_ref[0, rows], shift_ref, scale_ref, nw_ref)
            normalised.add(s)
        lanes = slice(g * gw, (g + 1) * gw)
        project = lambda block: _dot(
            hb_ref[rows], w_ref[:, block * d + g * gw:block * d + (g + 1) * gw])
        f_fwd = project(0)
        yield
        v = project(2).astype(BF16)
        v_ref[0, rows, lanes] = v
        yield
        q = _silu(project(3)) * (dh ** -0.5)
        q_ref[0, rows, lanes] = q.astype(BF16)
        yield
        fb_ref[0, rows, lanes] = project(1).astype(BF16)
        yield
        zg_ref[0, rows, lanes] = _silu(project(4)).astype(BF16)
        yield
        return f_fwd, v, q

    def emit(s, h, o):
        of_ref[0, _subtile_rows(s, FWD_SUBTILE), h * dh:(h + 1) * dh] = o.astype(BF16)

    subtiles = list(range(x_ref.shape[1] // FWD_SUBTILE))
    _interleave(_hgrn_items(0, subtiles, fetch, lg_ref, (cum_ref.at[0], keep_ref.at[0]), st_ref,
                            emit))


def _fwd_sweep(x, shift, scale, norm_w, w_in, lb_logits, cum, keep, s0_f):
    bsz, t, d = x.shape
    dh = d // HEADS
    tq = FWD_TOKENS_PER_STEP
    tile = pl.BlockSpec((1, tq, d), lambda b, i: (b, i, 0))
    row = pl.BlockSpec((1, 1, d), lambda b, i: (b, 0, 0))
    saved = jax.ShapeDtypeStruct((bsz, t, d), BF16)
    return pl.pallas_call(
        _fwd_body,
        grid=(bsz, t // tq),
        in_specs=[
            tile, row, row,
            pl.BlockSpec((1, d), lambda b, i: (0, 0)),
            _resident(w_in), _resident(lb_logits), _resident(cum), _resident(keep),
            pl.BlockSpec((1, HEADS, dh, dh), lambda b, i: (b, 0, 0, 0)),
        ],
        out_specs=[tile] * 5,
        out_shape=[saved] * 5,
        scratch_shapes=[pltpu.VMEM((HEADS, dh, dh), F32), pltpu.VMEM((tq, d), BF16)],
        compiler_params=_COMPILER_PARAMS,
        name="hgrn_fwd_sweep",
    )(x, shift, scale, norm_w, w_in, lb_logits, cum, keep, s0_f)


def _pool_stream(x1, shift_ref, scale_ref, gate_ref, nw_ref, wi_ref, band_ref, inv_ref, wg_ref,
                 ps_ref, wo_ref, fw_ref, hb_ref, act_ref, store):
    d = x1.shape[-1]
    gd = d // len(POOL_WINDOWS)
    hb_ref[...] = _normalised_rows(x1, shift_ref, scale_ref, nw_ref)

    def issue(g):
        u = _dot(hb_ref[...], wi_ref[:, g * gd:(g + 1) * gd])
        yield
        z = _dot(hb_ref[...], wi_ref[:, d + g * gd:d + (g + 1) * gd])
        yield
        return u, z

    def prepare(g, projected):
        u, z = projected
        return u, z, u.astype(BF16)

    def first(g, prepared):
        u, z, u_bf = prepared
        band = band_ref[g]
        mean = _by_mask_rows(lambda rows: _dot(band, rows) * inv_ref[g], u_bf)
        yield
        return g, u, z, mean

    def second(cont):
        g, u, z, mean = cont
        pooled = mean - u
        y = _dot(pooled.astype(BF16), wg_ref[g]) * ps_ref[:, g * gd:(g + 1) * gd]
        act_ref[:, g * gd:(g + 1) * gd] = (y * _silu(z)).astype(BF16)
        yield

    def finish_after(g):
        if g != len(POOL_WINDOWS) - 1:
            return None

        def finish():
            mix = _dot(act_ref[...], wo_ref[...])
            store(_rmsnorm(x1 + gate_ref[0] * mix, fw_ref[...]))

        return finish

    return _pipeline(list(range(len(POOL_WINDOWS))), issue, prepare, first, second, finish_after)


def _bwd_body(x_ref, of_ref, fb_ref, v_ref, q_ref, zg_ref, gate0_ref, lg_ref, cum_ref, keep_ref,
              s0_ref, gw_ref, wo_ref, shift1_ref, scale1_ref, gate1_ref, nw1_ref, wi_ref,
              band_ref, inv_ref, wg_ref, ps_ref, wo1_ref, fw_ref, out_ref, st_ref, act0_ref,
              hb1_ref, act1_ref):
    d = x_ref.shape[-1]
    dh = d // HEADS
    gw = HEADS_PER_GROUP * dh
    _load_initial_state(s0_ref, st_ref)
    masks = (cum_ref.at[1], keep_ref.at[1])

    def fetch(s, g):
        at = (0, _subtile_rows(s, BWD_SUBTILE), slice(g * gw, (g + 1) * gw))
        yield
        return fb_ref[at].astype(F32), v_ref[at], q_ref[at].astype(F32)

    def emit(s, h, o):
        at = (0, _subtile_rows(s, BWD_SUBTILE), slice(h * dh, (h + 1) * dh))
        o = _rmsnorm(of_ref[at].astype(F32) + o, gw_ref[:, at[2]])
        act0_ref[at[1:]] = (o * zg_ref[at].astype(F32)).astype(BF16)

    def all_layers(s):
        yield from _hgrn_items(1, [s], fetch, lg_ref, masks, st_ref, emit, order)
        rows = _subtile_rows(s, BWD_SUBTILE)
        x1 = x_ref[0, rows] + gate0_ref[0] * _dot(act0_ref[rows], wo_ref[...])
        yield

        def store(out):
            out_ref[0, rows] = out

        yield from _pool_stream(x1, shift1_ref, scale1_ref, gate1_ref, nw1_ref, wi_ref, band_ref,
                                inv_ref, wg_ref, ps_ref, wo1_ref, fw_ref, hb1_ref.at[rows],
                                act1_ref.at[rows], store)

    subtiles = list(range(x_ref.shape[1] // BWD_SUBTILE))[::-1]
    order = _StateOrder(subtiles)
    _interleave(*[all_layers(s) for s in subtiles], lag=STREAM_LAG)


def _bwd_sweep(x, saved, gate0, lb_logits, cum, keep, s0_b, gnorm_w, w_out, shift1, scale1, gate1,
               norm_w1, pool_w_in, band, inv_count, w_grp, pool_scale, pool_w_out, final_w):
    bsz, t, d = x.shape
    dh = d // HEADS
    tq = BWD_TOKENS_PER_STEP
    nt = t // tq
    tile = pl.BlockSpec((1, tq, d), lambda b, i: (b, nt - 1 - i, 0))
    row = pl.BlockSpec((1, 1, d), lambda b, i: (b, 0, 0))
    vec = pl.BlockSpec((1, d), lambda b, i: (0, 0))
    whole = _resident
    return pl.pallas_call(
        _bwd_body,
        grid=(bsz, nt),
        in_specs=[tile] * 6 + [
            row, whole(lb_logits), whole(cum), whole(keep),
            pl.BlockSpec((1, HEADS, dh, dh), lambda b, i: (b, 0, 0, 0)),
            vec, whole(w_out), row, row, row, vec, whole(pool_w_in), whole(band),
            whole(inv_count), whole(w_grp), vec, whole(pool_w_out), vec,
        ],
        out_specs=tile,
        out_shape=jax.ShapeDtypeStruct((bsz, t, d), F32),
        scratch_shapes=[pltpu.VMEM((HEADS, dh, dh), F32)] + [pltpu.VMEM((tq, d), BF16)] * 3,
        compiler_params=_COMPILER_PARAMS,
        name="hgrn_bwd_pool_sweep",
    )(x, *saved, gate0, lb_logits, cum, keep, s0_b, gnorm_w, w_out, shift1, scale1, gate1,
      norm_w1, pool_w_in, band, inv_count, w_grp, pool_scale, pool_w_out, final_w)


def _cumulative_matrices(n):
    r = jnp.arange(n)[:, None]
    c = jnp.arange(n)[None, :]
    same = (r // CHUNK) == (c // CHUNK)
    cum = jnp.stack([same & (c <= r), same & (c >= r)])
    return cum.astype(BF16), cum[:, :INTRA_ROWS, :INTRA_ROWS].astype(F32)


def _pool_windows(n):
    r = jnp.arange(n)[:, None]
    c = jnp.arange(n)[None, :]
    same = (r // GRID_W) == (c // GRID_W)
    band = jnp.stack([(same & (c >= r - w // 2) & (c < r - w // 2 + w)) for w in POOL_WINDOWS])
    count = jnp.sum(band.astype(F32), axis=-1, keepdims=True)
    return band.astype(BF16), 1.0 / count


def kernel(x, c, ctx, c_ctx, ada_w, ada_b, norm_w, hgrn_w_in, hgrn_lb_logits, hgrn_gnorm_w,
           hgrn_w_out, pool_w_in, pool_w_grp, pool_scale, pool_w_out, final_norm_w):
    bsz, t, d = x.shape
    depth = ada_w.shape[0]
    assert depth == 2 and hgrn_w_in.shape[0] == 1 and pool_w_in.shape[0] == 1
    assert t % FWD_TOKENS_PER_STEP == 0 and t % BWD_TOKENS_PER_STEP == 0
    assert ctx.shape[1] % MASK_ROWS == 0 and bsz < COND_ROWS
    assert bsz % CTX_BATCH_PER_STEP == 0 and ada_w.shape[2] % ADA_COLS_PER_STEP == 0
    assert FWD_TOKENS_PER_STEP % FWD_SUBTILE == 0 and BWD_TOKENS_PER_STEP % BWD_SUBTILE == 0
    assert FWD_SUBTILE % MASK_ROWS == 0 and BWD_SUBTILE % MASK_ROWS == 0
    assert MASK_ROWS % INTRA_ROWS == 0 and INTRA_ROWS % CHUNK == 0 and MASK_ROWS % GRID_W == 0

    cond = jnp.zeros((COND_ROWS, d), F32).at[:bsz].set(c).at[bsz].set(c_ctx)
    mods = _ada_mods(cond, ada_w, ada_b)
    shift, scale, gate = (mods[:, :, i * d:(i + 1) * d] for i in range(3))
    per_batch = lambda m, l: m[l, :bsz].reshape(bsz, 1, d)

    w_in = hgrn_w_in[0].astype(BF16)
    lb_logits = hgrn_lb_logits.reshape(hgrn_lb_logits.shape[0], 2, 1, d)
    cum, keep = _cumulative_matrices(MASK_ROWS)
    band, inv_count = _pool_windows(MASK_ROWS)

    s0_f, s0_b = _ctx_states(ctx, shift[0, bsz:bsz + 1], scale[0, bsz:bsz + 1], norm_w[0:1],
                             w_in, lb_logits, cum, keep)
    saved = _fwd_sweep(x, per_batch(shift, 0), per_batch(scale, 0), norm_w[0:1], w_in,
                       lb_logits, cum, keep, s0_f)
    return _bwd_sweep(x, saved, per_batch(gate, 0), lb_logits, cum, keep, s0_b, hgrn_gnorm_w[0:1],
                      hgrn_w_out[0].astype(BF16), per_batch(shift, 1), per_batch(scale, 1),
                      per_batch(gate, 1), norm_w[1:2], pool_w_in[0].astype(BF16), band, inv_count,
                      pool_w_grp[0].astype(BF16), pool_scale[0:1], pool_w_out[0].astype(BF16),
                      final_norm_w.reshape(1, d))
```

```python
import jax
import jax.numpy as jnp
from jax import lax
from jax.experimental import pallas as pl
from jax.experimental.pallas import tpu as pltpu

HEADS = 8
CHUNK = 64
GRID_W = 64
POOL_WINDOWS = (2, 4, 8, 16)
EPS = 1e-6
MASK_ROWS = 256
FWD_SUBTILE = 512
BWD_SUBTILE = 256
FWD_TOKENS_PER_STEP = 1024
BWD_TOKENS_PER_STEP = 512
HEADS_PER_GROUP = 2
INTRA_ROWS = 128
LOOKAHEAD = 2
STREAM_LAG = 20
CTX_BATCH_PER_STEP = 4
COND_ROWS = 16
VMEM_LIMIT_BYTES = 56 * 1024 * 1024

F32 = jnp.float32
BF16 = jnp.bfloat16

_COMPILER_PARAMS = pltpu.CompilerParams(vmem_limit_bytes=VMEM_LIMIT_BYTES)


def _silu(a):
    half = 0.5 * a
    return half + half * jnp.tanh(half)


def _rmsnorm(xf, w):
    y = xf * lax.rsqrt(jnp.mean(xf * xf, axis=-1, keepdims=True) + EPS)
    return y * w


def _normalised_rows(x, shift_ref, scale_ref, nw_ref):
    gain = nw_ref[...] * (1.0 + scale_ref[0])
    return (_rmsnorm(x, gain) + shift_ref[0]).astype(BF16)


def _dot(a, b):
    return jnp.dot(a, b, preferred_element_type=F32)


def _dot_nt(a, b):
    return lax.dot_general(a, b, (((1,), (1,)), ((), ())), preferred_element_type=F32)


def _dot_tn(a, b):
    return lax.dot_general(a, b, (((0,), (0,)), ((), ())), preferred_element_type=F32)


def _lower_bound(logits, layer):
    rows = [logits[n] for n in range(logits.shape[0])]
    m = jnp.zeros_like(rows[0])
    for r in rows:
        m = jnp.maximum(m, r)
    exps = [jnp.exp(r - m) for r in rows]
    denom = jnp.exp(-m)
    for e in exps:
        denom = denom + e
    acc = exps[0]
    for e in exps[1:layer + 1]:
        acc = acc + e
    return acc / denom


def _subtile_rows(s, size):
    return slice(s * size, (s + 1) * size)


def _by_mask_rows(fn, *arrays):
    n = arrays[0].shape[0] // MASK_ROWS
    blocks = [fn(*(a[i * MASK_ROWS:(i + 1) * MASK_ROWS] for a in arrays)) for i in range(n)]
    return blocks[0] if n == 1 else jnp.concatenate(blocks, axis=0)


def _resident(a):
    return pl.BlockSpec(a.shape, lambda b, i: (0,) * a.ndim, pipeline_mode=pl.Buffered(1))


class _Steps:
    def __init__(self, gen):
        self._gen = gen
        self.done = False
        self.value = None

    def step(self):
        if not self.done:
            try:
                next(self._gen)
            except StopIteration as stop:
                self.done = True
                self.value = stop.value
        return not self.done


def _alongside(main, side):
    while main.step():
        yield
        if side.step():
            yield
    return main.value


def _pipeline(items, issue, prepare, first, second, finish_after):
    ahead = min(LOOKAHEAD, len(items))
    issued = []
    for item in items[:ahead]:
        issued.append((yield from issue(item)))
    prepared = []
    for item, handle in zip(items, issued):
        prepared.append(prepare(item, handle))
        yield
    deferred = []
    for i, item in enumerate(items):
        later = items[i + ahead] if i + ahead < len(items) else None
        issuing = _Steps(issue(later) if later is not None else iter(()))
        cont = yield from _alongside(_Steps(first(item, prepared[i])), issuing)
        for fn in deferred:
            fn()
            yield
        deferred = []
        yield from _alongside(_Steps(second(cont)), issuing)
        while issuing.step():
            yield
        fn = finish_after(item)
        if fn is not None:
            deferred.append(fn)
        if later is not None:
            prepared.append(prepare(later, issuing.value))
            yield
    for fn in deferred:
        fn()
        yield


def _interleave(*streams, lag=0):
    done = [False] * len(streams)
    rnd = 0
    while not all(done):
        for k, stream in enumerate(streams):
            if done[k] or rnd < k * lag:
                continue
            try:
                next(stream)
            except StopIteration:
                done[k] = True
        rnd += 1


def _head_operands(reverse, b, k, q):
    nc = b.shape[0] // CHUNK
    mid = CHUNK // 2 if reverse else CHUNK // 2 - 1
    last = 0 if reverse else CHUNK - 1
    k_dec, q_dec, k_end, q_in, decay = [], [], [], [], []
    for n in range(nc):
        rows = slice(n * CHUNK, (n + 1) * CHUNK)
        bn = b[rows]
        b_mid = bn[mid:mid + 1]
        b_last = bn[last:last + 1]
        kd = k[rows] * jnp.exp2(b_mid - bn)
        k_dec.append(kd.astype(BF16))
        k_end.append((kd * jnp.exp2(b_last - b_mid)).astype(BF16))
        decay.append(jnp.exp2(b_last))
        if q is not None:
            qd = q[rows] * jnp.exp2(bn - b_mid)
            q_dec.append(qd.astype(BF16))
            q_in.append((qd * jnp.exp2(b_mid)).astype(BF16))
    return k_dec, q_dec, k_end, q_in, decay


def _scan_head(reverse, operands, v, keep, st):
    k_dec, q_dec, k_end, q_in, decay = operands
    nc = len(decay)
    per_block = INTRA_ROWS // CHUNK
    scores = []
    if q_dec:
        for j in range(nc // per_block):
            blk = slice(j * per_block, (j + 1) * per_block)
            scores.append(_dot_nt(jnp.concatenate(q_dec[blk], axis=0),
                                  jnp.concatenate(k_dec[blk], axis=0)))
        yield
    update = [_dot_tn(v[n * CHUNK:(n + 1) * CHUNK], k_end[n]) for n in range(nc)]
    yield
    if q_dec:
        scores = [jnp.where(keep, a, 0.0).astype(BF16) for a in scores]
        o_intra = [_dot(a, v[j * INTRA_ROWS:(j + 1) * INTRA_ROWS]) for j, a in enumerate(scores)]
        yield
    carried = [None] * nc
    for n in (range(nc - 1, -1, -1) if reverse else range(nc)):
        carried[n] = st.astype(BF16)
        st = st * decay[n] + update[n]
    if not q_dec:
        return None, st
    o_inter = [_dot_nt(q_in[n], carried[n]) for n in range(nc)]
    return jnp.concatenate(o_intra, axis=0) + jnp.concatenate(o_inter, axis=0), st


class _StateOrder:
    def __init__(self, subtiles):
        self._before = {s: subtiles[:i] for i, s in enumerate(subtiles)}
        self._written = set()

    def read(self, s, h):
        missing = [p for p in self._before[s] if (p, h) not in self._written]
        assert not missing, f"state of head {h} read for subtile {s} before {missing} wrote it"

    def wrote(self, s, h):
        self._written.add((s, h))


def _hgrn_items(direction, subtiles, fetch, lg_ref, masks, st_ref, emit, order=None):
    dh = st_ref.shape[-1]
    gw = HEADS_PER_GROUP * dh
    reverse = direction == 1
    cum_ref, keep_ref = masks
    cum = cum_ref[...]
    keep = keep_ref[...] != 0.0
    lb_all = _lower_bound(lg_ref[:, direction], 0)

    def issue(item):
        return (yield from fetch(*item))

    def prepare(item, fetched):
        s, g = item
        f_pre, v, q = fetched
        lb = lb_all[:, g * gw:(g + 1) * gw]
        f = 0.5 * (1.0 + lb) + (0.5 * (1.0 - lb)) * jnp.tanh(0.5 * f_pre)
        b = _by_mask_rows(lambda logf: _dot(cum, logf), jnp.log2(f).astype(BF16))
        k = 1.0 - f
        operands = []
        for hh in range(HEADS_PER_GROUP):
            lanes = slice(hh * dh, (hh + 1) * dh)
            operands.append(_head_operands(reverse, b[:, lanes], k[:, lanes],
                                           None if q is None else q[:, lanes]))
        return operands, v, q is not None

    def first(item, prepared):
        s, g = item
        operands, v, has_q = prepared
        heads = []
        for hh in range(HEADS_PER_GROUP):
            lanes = slice(hh * dh, (hh + 1) * dh)
            if order is not None:
                order.read(s, g * HEADS_PER_GROUP + hh)
            head = _Steps(_scan_head(reverse, operands[hh], v[:, lanes], keep,
                                     st_ref[g * HEADS_PER_GROUP + hh]))
            for _ in range(2 if has_q else 1):
                head.step()
                yield
            heads.append(head)
        return item, heads

    def second(cont):
        (s, g), heads = cont
        for hh, head in enumerate(heads):
            h = g * HEADS_PER_GROUP + hh
            while head.step():
                yield
            o, st = head.value
            st_ref[h] = st
            if order is not None:
                order.wrote(s, h)
            if emit is not None:
                emit(s, h, o)
            yield

    items = [(s, g) for s in subtiles for g in range(HEADS // HEADS_PER_GROUP)]
    return _pipeline(items, issue, prepare, first, second, lambda item: None)


def _ada_body(cond_ref, w_ref, b_ref, o_ref):
    o_ref[0, 0] = _dot(_silu(cond_ref[...]), w_ref[0]) + b_ref[0]


def _ada_mods(cond, ada_w, ada_b):
    depth, d, n = ada_w.shape
    return pl.pallas_call(
        _ada_body,
        grid=(depth, n // d),
        in_specs=[
            pl.BlockSpec((COND_ROWS, d), lambda l, j: (0, 0)),
            pl.BlockSpec((1, d, d), lambda l, j: (l, 0, j)),
            pl.BlockSpec((1, 1, d), lambda l, j: (l, 0, j)),
        ],
        out_specs=pl.BlockSpec((1, 1, COND_ROWS, d), lambda l, j: (l, j, 0, 0)),
        out_shape=jax.ShapeDtypeStruct((depth, n // d, COND_ROWS, d), F32),
        compiler_params=_COMPILER_PARAMS,
        name="ada_mods",
    )(cond, ada_w, ada_b.reshape(depth, 1, n))


def _mod_spec(mods, layer, which, row=None):
    base = (layer * 3 + which) * COND_ROWS
    if row is None:
        return pl.BlockSpec((1, 1, mods.shape[-1]), lambda b, *_: (base + b, 0, 0))
    return pl.BlockSpec((1, 1, mods.shape[-1]), lambda *_: (base + row, 0, 0))


def _ctx_body(ctx_ref, shift_ref, scale_ref, nw_ref, w_ref, lg_ref, cum_ref, keep_ref,
              sf_ref, sb_ref, hb_ref):
    n_batch, tc, d = ctx_ref.shape
    gw = HEADS_PER_GROUP * (d // HEADS)
    subtiles = list(range(tc // MASK_ROWS))
    normalised, v = set(), {}

    def fetch(direction, bi, s, g):
        rows = _subtile_rows(s, MASK_ROWS)
        hb = hb_ref.at[bi]
        if (bi, s) not in normalised:
            hb[rows] = _normalised_rows(ctx_ref[bi, rows], shift_ref, scale_ref, nw_ref)
            normalised.add((bi, s))
        cols = lambda block: slice(block * d + g * gw, block * d + (g + 1) * gw)
        if (bi, s, g) not in v:
            v[bi, s, g] = _dot(hb[rows], w_ref[:, cols(2)]).astype(BF16)
            yield
        f_pre = _dot(hb[rows], w_ref[:, cols(direction)])
        yield
        return f_pre, v[bi, s, g], None

    streams = []
    for bi in range(n_batch):
        for direction, st_ref in enumerate((sf_ref, sb_ref)):
            st = st_ref.at[bi]
            st[...] = jnp.zeros(st.shape, F32)
            order = subtiles[::-1] if direction else subtiles
            streams.append(_hgrn_items(
                direction, order, lambda s, g, dr=direction, bi=bi: fetch(dr, bi, s, g), lg_ref,
                (cum_ref.at[direction], keep_ref.at[direction]), st, None))
    _interleave(*streams)


def _ctx_states(ctx, mods, ctx_row, norm_w, w_in, lb_logits, cum, keep):
    bsz, tc, d = ctx.shape
    dh = d // HEADS
    const2 = lambda b: (0, 0)
    nb = CTX_BATCH_PER_STEP
    state_spec = pl.BlockSpec((nb, HEADS, dh, dh), lambda b: (b, 0, 0, 0))
    state_shape = jax.ShapeDtypeStruct((bsz, HEADS, dh, dh), F32)
    return pl.pallas_call(
        _ctx_body,
        grid=(bsz // nb,),
        in_specs=[
            pl.BlockSpec((nb, tc, d), lambda b: (b, 0, 0)),
            _mod_spec(mods, 0, 0, ctx_row), _mod_spec(mods, 0, 1, ctx_row),
            pl.BlockSpec((1, d), const2),
            pl.BlockSpec((d, 3 * d), const2),
            pl.BlockSpec(lb_logits.shape, lambda b: (0, 0, 0, 0)),
            pl.BlockSpec(cum.shape, lambda b: (0, 0, 0)),
            pl.BlockSpec(keep.shape, lambda b: (0, 0, 0)),
        ],
        out_specs=[state_spec, state_spec],
        out_shape=[state_shape, state_shape],
        scratch_shapes=[pltpu.VMEM((nb, tc, d), BF16)],
        compiler_params=_COMPILER_PARAMS,
        name="ctx_states",
    )(ctx, mods, mods, norm_w, w_in, lb_logits, cum, keep)


def _load_initial_state(s0_ref, st_ref):
    @pl.when(pl.program_id(1) == 0)
    def _():
        st_ref[...] = s0_ref[0]


def _fwd_body(x_ref, shift_ref, scale_ref, nw_ref, w_ref, lg_ref, cum_ref, keep_ref, s0_ref,
              of_ref, fb_ref, v_ref, q_ref, zg_ref, st_ref, hb_ref):
    d = x_ref.shape[-1]
    dh = d // HEADS
    gw = HEADS_PER_GROUP * dh
    _load_initial_state(s0_ref, st_ref)
    normalised = set()

    def fetch(s, g):
        rows = _subtile_rows(s, FWD_SUBTILE)
        if s not in normalised:
            hb_ref[rows] = _normalised_rows(x_ref[0, rows], shift_ref, scale_ref, nw_ref)
            normalised.add(s)
        lanes = slice(g * gw, (g + 1) * gw)
        project = lambda block: _dot(
            hb_ref[rows], w_ref[:, block * d + g * gw:block * d + (g + 1) * gw])
        f_fwd = project(0)
        yield
        v = project(2).astype(BF16)
        v_ref[0, rows, lanes] = v
        yield
        q = _silu(project(3)) * (dh ** -0.5)
        q_ref[0, rows, lanes] = q.astype(BF16)
        yield
        fb_ref[0, rows, lanes] = project(1).astype(BF16)
        yield
        zg_ref[0, rows, lanes] = _silu(project(4)).astype(BF16)
        yield
        return f_fwd, v, q

    def emit(s, h, o):
        of_ref[0, _subtile_rows(s, FWD_SUBTILE), h * dh:(h + 1) * dh] = o.astype(BF16)

    subtiles = list(range(x_ref.shape[1] // FWD_SUBTILE))
    _interleave(_hgrn_items(0, subtiles, fetch, lg_ref, (cum_ref.at[0], keep_ref.at[0]), st_ref,
                            emit))


def _fwd_sweep(x, mods, norm_w, w_in, lb_logits, cum, keep, s0_f):
    bsz, t, d = x.shape
    dh = d // HEADS
    tq = FWD_TOKENS_PER_STEP
    tile = pl.BlockSpec((1, tq, d), lambda b, i: (b, i, 0))
    saved = jax.ShapeDtypeStruct((bsz, t, d), BF16)
    return pl.pallas_call(
        _fwd_body,
        grid=(bsz, t // tq),
        in_specs=[
            tile, _mod_spec(mods, 0, 0), _mod_spec(mods, 0, 1),
            pl.BlockSpec((1, d), lambda b, i: (0, 0)),
            _resident(w_in), _resident(lb_logits), _resident(cum), _resident(keep),
            pl.BlockSpec((1, HEADS, dh, dh), lambda b, i: (b, 0, 0, 0)),
        ],
        out_specs=[tile] * 5,
        out_shape=[saved] * 5,
        scratch_shapes=[pltpu.VMEM((HEADS, dh, dh), F32), pltpu.VMEM((tq, d), BF16)],
        compiler_params=_COMPILER_PARAMS,
        name="hgrn_fwd_sweep",
    )(x, mods, mods, norm_w, w_in, lb_logits, cum, keep, s0_f)


def _pool_stream(x1, shift_ref, scale_ref, gate_ref, nw_ref, wi_ref, band_ref, inv_ref, wg_ref,
                 ps_ref, wo_ref, fw_ref, hb_ref, act_ref, store):
    d = x1.shape[-1]
    gd = d // len(POOL_WINDOWS)
    hb_ref[...] = _normalised_rows(x1, shift_ref, scale_ref, nw_ref)

    def issue(g):
        u = _dot(hb_ref[...], wi_ref[:, g * gd:(g + 1) * gd])
        yield
        z = _dot(hb_ref[...], wi_ref[:, d + g * gd:d + (g + 1) * gd])
        yield
        return u, z

    def prepare(g, projected):
        u, z = projected
        return u, z, u.astype(BF16)

    def first(g, prepared):
        u, z, u_bf = prepared
        band = band_ref[g]
        mean = _by_mask_rows(lambda rows: _dot(band, rows) * inv_ref[g], u_bf)
        yield
        return g, u, z, mean

    def second(cont):
        g, u, z, mean = cont
        pooled = mean - u
        y = _dot(pooled.astype(BF16), wg_ref[g]) * ps_ref[:, g * gd:(g + 1) * gd]
        act_ref[:, g * gd:(g + 1) * gd] = (y * _silu(z)).astype(BF16)
        yield

    def finish_after(g):
        if g != len(POOL_WINDOWS) - 1:
            return None

        def finish():
            mix = _dot(act_ref[...], wo_ref[...])
            store(_rmsnorm(x1 + gate_ref[0] * mix, fw_ref[...]))

        return finish

    return _pipeline(list(range(len(POOL_WINDOWS))), issue, prepare, first, second, finish_after)


def _bwd_body(x_ref, of_ref, fb_ref, v_ref, q_ref, zg_ref, gate0_ref, lg_ref, cum_ref, keep_ref,
              s0_ref, gw_ref, wo_ref, shift1_ref, scale1_ref, gate1_ref, nw1_ref, wi_ref,
              band_ref, inv_ref, wg_ref, ps_ref, wo1_ref, fw_ref, out_ref, st_ref, act0_ref,
              hb1_ref, act1_ref):
    d = x_ref.shape[-1]
    dh = d // HEADS
    gw = HEADS_PER_GROUP * dh
    _load_initial_state(s0_ref, st_ref)
    masks = (cum_ref.at[1], keep_ref.at[1])

    def fetch(s, g):
        at = (0, _subtile_rows(s, BWD_SUBTILE), slice(g * gw, (g + 1) * gw))
        yield
        return fb_ref[at].astype(F32), v_ref[at], q_ref[at].astype(F32)

    def emit(s, h, o):
        at = (0, _subtile_rows(s, BWD_SUBTILE), slice(h * dh, (h + 1) * dh))
        o = _rmsnorm(of_ref[at].astype(F32) + o, gw_ref[:, at[2]])
        act0_ref[at[1:]] = (o * zg_ref[at].astype(F32)).astype(BF16)

    def all_layers(s):
        yield from _hgrn_items(1, [s], fetch, lg_ref, masks, st_ref, emit, order)
        rows = _subtile_rows(s, BWD_SUBTILE)
        x1 = x_ref[0, rows] + gate0_ref[0] * _dot(act0_ref[rows], wo_ref[...])
        yield

        def store(out):
            out_ref[0, rows] = out

        yield from _pool_stream(x1, shift1_ref, scale1_ref, gate1_ref, nw1_ref, wi_ref, band_ref,
                                inv_ref, wg_ref, ps_ref, wo1_ref, fw_ref, hb1_ref.at[rows],
                                act1_ref.at[rows], store)

    subtiles = list(range(x_ref.shape[1] // BWD_SUBTILE))[::-1]
    order = _StateOrder(subtiles)
    _interleave(*[all_layers(s) for s in subtiles], lag=STREAM_LAG)


def _bwd_sweep(x, saved, mods, lb_logits, cum, keep, s0_b, gnorm_w, w_out, norm_w1, pool_w_in,
               band, inv_count, w_grp, pool_scale, pool_w_out, final_w):
    bsz, t, d = x.shape
    dh = d // HEADS
    tq = BWD_TOKENS_PER_STEP
    nt = t // tq
    tile = pl.BlockSpec((1, tq, d), lambda b, i: (b, nt - 1 - i, 0))
    vec = pl.BlockSpec((1, d), lambda b, i: (0, 0))
    whole = _resident
    return pl.pallas_call(
        _bwd_body,
        grid=(bsz, nt),
        in_specs=[tile] * 6 + [
            _mod_spec(mods, 0, 2), whole(lb_logits), whole(cum), whole(keep),
            pl.BlockSpec((1, HEADS, dh, dh), lambda b, i: (b, 0, 0, 0)),
            vec, whole(w_out), _mod_spec(mods, 1, 0), _mod_spec(mods, 1, 1), _mod_spec(mods, 1, 2),
            vec, whole(pool_w_in), whole(band),
            whole(inv_count), whole(w_grp), vec, whole(pool_w_out), vec,
        ],
        out_specs=tile,
        out_shape=jax.ShapeDtypeStruct((bsz, t, d), F32),
        scratch_shapes=[pltpu.VMEM((HEADS, dh, dh), F32)] + [pltpu.VMEM((tq, d), BF16)] * 3,
        compiler_params=_COMPILER_PARAMS,
        name="hgrn_bwd_pool_sweep",
    )(x, *saved, mods, lb_logits, cum, keep, s0_b, gnorm_w, w_out, mods, mods, mods,
      norm_w1, pool_w_in, band, inv_count, w_grp, pool_scale, pool_w_out, final_w)


def _cumulative_matrices(n):
    r = jnp.arange(n)[:, None]
    c = jnp.arange(n)[None, :]
    same = (r // CHUNK) == (c // CHUNK)
    cum = jnp.stack([same & (c <= r), same & (c >= r)])
    return cum.astype(BF16), cum[:, :INTRA_ROWS, :INTRA_ROWS].astype(F32)


def _pool_windows(n):
    r = jnp.arange(n)[:, None]
    c = jnp.arange(n)[None, :]
    same = (r // GRID_W) == (c // GRID_W)
    band = jnp.stack([(same & (c >= r - w // 2) & (c < r - w // 2 + w)) for w in POOL_WINDOWS])
    count = jnp.sum(band.astype(F32), axis=-1, keepdims=True)
    return band.astype(BF16), 1.0 / count


def kernel(x, c, ctx, c_ctx, ada_w, ada_b, norm_w, hgrn_w_in, hgrn_lb_logits, hgrn_gnorm_w,
           hgrn_w_out, pool_w_in, pool_w_grp, pool_scale, pool_w_out, final_norm_w):
    bsz, t, d = x.shape
    depth = ada_w.shape[0]
    assert depth == 2 and hgrn_w_in.shape[0] == 1 and pool_w_in.shape[0] == 1
    assert t % FWD_TOKENS_PER_STEP == 0 and t % BWD_TOKENS_PER_STEP == 0
    assert ctx.shape[1] % MASK_ROWS == 0 and bsz < COND_ROWS
    assert bsz % CTX_BATCH_PER_STEP == 0 and ada_w.shape[2] == 3 * d
    assert FWD_TOKENS_PER_STEP % FWD_SUBTILE == 0 and BWD_TOKENS_PER_STEP % BWD_SUBTILE == 0
    assert FWD_SUBTILE % MASK_ROWS == 0 and BWD_SUBTILE % MASK_ROWS == 0
    assert MASK_ROWS % INTRA_ROWS == 0 and INTRA_ROWS % CHUNK == 0 and MASK_ROWS % GRID_W == 0

    cond = jnp.zeros((COND_ROWS, d), F32).at[:bsz].set(c).at[bsz].set(c_ctx)
    mods = _ada_mods(cond, ada_w, ada_b).reshape(depth * 3 * COND_ROWS, 1, d)

    w_in = hgrn_w_in[0].astype(BF16)
    lb_logits = hgrn_lb_logits.reshape(hgrn_lb_logits.shape[0], 2, 1, d)
    cum, keep = _cumulative_matrices(MASK_ROWS)
    band, inv_count = _pool_windows(MASK_ROWS)

    s0_f, s0_b = _ctx_states(ctx, mods, bsz, norm_w[0:1], w_in, lb_logits, cum, keep)
    saved = _fwd_sweep(x, mods, norm_w[0:1], w_in, lb_logits, cum, keep, s0_f)
    return _bwd_sweep(x, saved, mods, lb_logits, cum, keep, s0_b, hgrn_gnorm_w[0:1],
                      hgrn_w_out[0].astype(BF16), norm_w[1:2], pool_w_in[0].astype(BF16), band,
                      inv_count, pool_w_grp[0].astype(BF16), pool_scale[0:1],
                      pool_w_out[0].astype(BF16), final_norm_w.reshape(1, d))
```

```python
import jax
import jax.numpy as jnp
from jax import lax
from jax.experimental import pallas as pl
from jax.experimental.pallas import tpu as pltpu

HEADS = 8
CHUNK = 64
GRID_W = 64
POOL_WINDOWS = (2, 4, 8, 16)
EPS = 1e-6
MASK_ROWS = 256
FWD_SUBTILE = 512
BWD_SUBTILE = 256
FWD_TOKENS_PER_STEP = 1024
BWD_TOKENS_PER_STEP = 1024
HEADS_PER_GROUP = 2
INTRA_ROWS = 128
LOOKAHEAD = 2
STREAM_LAG = 20
CTX_BATCH_PER_STEP = 4
COND_ROWS = 16
VMEM_LIMIT_BYTES = 60 * 1024 * 1024

F32 = jnp.float32
BF16 = jnp.bfloat16

_COMPILER_PARAMS = pltpu.CompilerParams(vmem_limit_bytes=VMEM_LIMIT_BYTES)


def _silu(a):
    half = 0.5 * a
    return half + half * jnp.tanh(half)


def _rmsnorm(xf, w):
    y = xf * lax.rsqrt(jnp.mean(xf * xf, axis=-1, keepdims=True) + EPS)
    return y * w


def _normalised_rows(x, shift_ref, scale_ref, nw_ref):
    gain = nw_ref[...] * (1.0 + scale_ref[0])
    return (_rmsnorm(x, gain) + shift_ref[0]).astype(BF16)


def _dot(a, b):
    return jnp.dot(a, b, preferred_element_type=F32)


def _dot_nt(a, b):
    return lax.dot_general(a, b, (((1,), (1,)), ((), ())), preferred_element_type=F32)


def _dot_tn(a, b):
    return lax.dot_general(a, b, (((0,), (0,)), ((), ())), preferred_element_type=F32)


def _lower_bound(logits, layer):
    rows = [logits[n] for n in range(logits.shape[0])]
    m = jnp.zeros_like(rows[0])
    for r in rows:
        m = jnp.maximum(m, r)
    exps = [jnp.exp(r - m) for r in rows]
    denom = jnp.exp(-m)
    for e in exps:
        denom = denom + e
    acc = exps[0]
    for e in exps[1:layer + 1]:
        acc = acc + e
    return acc / denom


def _subtile_rows(s, size):
    return slice(s * size, (s + 1) * size)


def _by_mask_rows(fn, *arrays):
    n = arrays[0].shape[0] // MASK_ROWS
    blocks = [fn(*(a[i * MASK_ROWS:(i + 1) * MASK_ROWS] for a in arrays)) for i in range(n)]
    return blocks[0] if n == 1 else jnp.concatenate(blocks, axis=0)


def _resident(a):
    return pl.BlockSpec(a.shape, lambda b, i: (0,) * a.ndim, pipeline_mode=pl.Buffered(1))


class _Steps:
    def __init__(self, gen):
        self._gen = gen
        self.done = False
        self.value = None

    def step(self):
        if not self.done:
            try:
                next(self._gen)
            except StopIteration as stop:
                self.done = True
                self.value = stop.value
        return not self.done


def _alongside(main, side):
    while main.step():
        yield
        if side.step():
            yield
    return main.value


def _pipeline(items, issue, prepare, first, second, finish_after):
    ahead = min(LOOKAHEAD, len(items))
    issued = []
    for item in items[:ahead]:
        issued.append((yield from issue(item)))
    prepared = []
    for item, handle in zip(items, issued):
        prepared.append(prepare(item, handle))
        yield
    deferred = []
    for i, item in enumerate(items):
        later = items[i + ahead] if i + ahead < len(items) else None
        issuing = _Steps(issue(later) if later is not None else iter(()))
        cont = yield from _alongside(_Steps(first(item, prepared[i])), issuing)
        for fn in deferred:
            fn()
            yield
        deferred = []
        yield from _alongside(_Steps(second(cont)), issuing)
        while issuing.step():
            yield
        fn = finish_after(item)
        if fn is not None:
            deferred.append(fn)
        if later is not None:
            prepared.append(prepare(later, issuing.value))
            yield
    for fn in deferred:
        fn()
        yield


def _interleave(*streams, lag=0):
    done = [False] * len(streams)
    rnd = 0
    while not all(done):
        for k, stream in enumerate(streams):
            if done[k] or rnd < k * lag:
                continue
            try:
                next(stream)
            except StopIteration:
                done[k] = True
        rnd += 1


def _head_operands(reverse, b, k, q):
    nc = b.shape[0] // CHUNK
    mid = CHUNK // 2 if reverse else CHUNK // 2 - 1
    last = 0 if reverse else CHUNK - 1
    k_dec, q_dec, k_end, q_in, decay = [], [], [], [], []
    for n in range(nc):
        rows = slice(n * CHUNK, (n + 1) * CHUNK)
        bn = b[rows]
        b_mid = bn[mid:mid + 1]
        b_last = bn[last:last + 1]
        kd = k[rows] * jnp.exp2(b_mid - bn)
        k_dec.append(kd.astype(BF16))
        k_end.append((kd * jnp.exp2(b_last - b_mid)).astype(BF16))
        decay.append(jnp.exp2(b_last))
        if q is not None:
            qd = q[rows] * jnp.exp2(bn - b_mid)
            q_dec.append(qd.astype(BF16))
            q_in.append((qd * jnp.exp2(b_mid)).astype(BF16))
    return k_dec, q_dec, k_end, q_in, decay


def _scan_head(reverse, operands, v, keep, st):
    k_dec, q_dec, k_end, q_in, decay = operands
    nc = len(decay)
    per_block = INTRA_ROWS // CHUNK
    scores = []
    if q_dec:
        for j in range(nc // per_block):
            blk = slice(j * per_block, (j + 1) * per_block)
            scores.append(_dot_nt(jnp.concatenate(q_dec[blk], axis=0),
                                  jnp.concatenate(k_dec[blk], axis=0)))
        yield
    update = [_dot_tn(v[n * CHUNK:(n + 1) * CHUNK], k_end[n]) for n in range(nc)]
    yield
    if q_dec:
        scores = [jnp.where(keep, a, 0.0).astype(BF16) for a in scores]
        o_intra = [_dot(a, v[j * INTRA_ROWS:(j + 1) * INTRA_ROWS]) for j, a in enumerate(scores)]
        yield
    carried = [None] * nc
    for n in (range(nc - 1, -1, -1) if reverse else range(nc)):
        carried[n] = st.astype(BF16)
        st = st * decay[n] + update[n]
    if not q_dec:
        return None, st
    o_inter = [_dot_nt(q_in[n], carried[n]) for n in range(nc)]
    return jnp.concatenate(o_intra, axis=0) + jnp.concatenate(o_inter, axis=0), st


class _StateOrder:
    def __init__(self, subtiles):
        self._before = {s: subtiles[:i] for i, s in enumerate(subtiles)}
        self._written = set()

    def read(self, s, h):
        missing = [p for p in self._before[s] if (p, h) not in self._written]
        assert not missing, f"state of head {h} read for subtile {s} before {missing} wrote it"

    def wrote(self, s, h):
        self._written.add((s, h))


def _hgrn_items(direction, subtiles, fetch, lg_ref, masks, st_ref, emit, order=None):
    dh = st_ref.shape[-1]
    gw = HEADS_PER_GROUP * dh
    reverse = direction == 1
    cum_ref, keep_ref = masks
    cum = cum_ref[...]
    keep = keep_ref[...] != 0.0
    lb_all = _lower_bound(lg_ref[:, direction], 0)

    def issue(item):
        return (yield from fetch(*item))

    def prepare(item, fetched):
        s, g = item
        f_pre, v, q = fetched
        lb = lb_all[:, g * gw:(g + 1) * gw]
        f = 0.5 * (1.0 + lb) + (0.5 * (1.0 - lb)) * jnp.tanh(0.5 * f_pre)
        b = _by_mask_rows(lambda logf: _dot(cum, logf), jnp.log2(f).astype(BF16))
        k = 1.0 - f
        operands = []
        for hh in range(HEADS_PER_GROUP):
            lanes = slice(hh * dh, (hh + 1) * dh)
            operands.append(_head_operands(reverse, b[:, lanes], k[:, lanes],
                                           None if q is None else q[:, lanes]))
        return operands, v, q is not None

    def first(item, prepared):
        s, g = item
        operands, v, has_q = prepared
        heads = []
        for hh in range(HEADS_PER_GROUP):
            lanes = slice(hh * dh, (hh + 1) * dh)
            if order is not None:
                order.read(s, g * HEADS_PER_GROUP + hh)
            head = _Steps(_scan_head(reverse, operands[hh], v[:, lanes], keep,
                                     st_ref[g * HEADS_PER_GROUP + hh]))
            for _ in range(2 if has_q else 1):
                head.step()
                yield
            heads.append(head)
        return item, heads

    def second(cont):
        (s, g), heads = cont
        for hh, head in enumerate(heads):
            h = g * HEADS_PER_GROUP + hh
            while head.step():
                yield
            o, st = head.value
            st_ref[h] = st
            if order is not None:
                order.wrote(s, h)
            if emit is not None:
                emit(s, h, o)
            yield

    items = [(s, g) for s in subtiles for g in range(HEADS // HEADS_PER_GROUP)]
    return _pipeline(items, issue, prepare, first, second, lambda item: None)


def _ada_body(cond_ref, w_ref, b_ref, o_ref):
    o_ref[0, 0] = _dot(_silu(cond_ref[...]), w_ref[0]) + b_ref[0]


def _ada_mods(cond, ada_w, ada_b):
    depth, d, n = ada_w.shape
    return pl.pallas_call(
        _ada_body,
        grid=(depth, n // d),
        in_specs=[
            pl.BlockSpec((COND_ROWS, d), lambda l, j: (0, 0)),
            pl.BlockSpec((1, d, d), lambda l, j: (l, 0, j)),
            pl.BlockSpec((1, 1, d), lambda l, j: (l, 0, j)),
        ],
        out_specs=pl.BlockSpec((1, 1, COND_ROWS, d), lambda l, j: (l, j, 0, 0)),
        out_shape=jax.ShapeDtypeStruct((depth, n // d, COND_ROWS, d), F32),
        compiler_params=_COMPILER_PARAMS,
        name="ada_mods",
    )(cond, ada_w, ada_b.reshape(depth, 1, n))


def _mod_spec(mods, layer, which, row=None):
    base = (layer * 3 + which) * COND_ROWS
    if row is None:
        return pl.BlockSpec((1, 1, mods.shape[-1]), lambda b, *_: (base + b, 0, 0))
    return pl.BlockSpec((1, 1, mods.shape[-1]), lambda *_: (base + row, 0, 0))


def _ctx_body(ctx_ref, shift_ref, scale_ref, nw_ref, w_ref, lg_ref, cum_ref, keep_ref,
              sf_ref, sb_ref, hb_ref):
    n_batch, tc, d = ctx_ref.shape
    gw = HEADS_PER_GROUP * (d // HEADS)
    subtiles = list(range(tc // MASK_ROWS))
    normalised, v = set(), {}

    def fetch(direction, bi, s, g):
        rows = _subtile_rows(s, MASK_ROWS)
        hb = hb_ref.at[bi]
        if (bi, s) not in normalised:
            hb[rows] = _normalised_rows(ctx_ref[bi, rows], shift_ref, scale_ref, nw_ref)
            normalised.add((bi, s))
        cols = lambda block: slice(block * d + g * gw, block * d + (g + 1) * gw)
        if (bi, s, g) not in v:
            v[bi, s, g] = _dot(hb[rows], w_ref[:, cols(2)]).astype(BF16)
            yield
        f_pre = _dot(hb[rows], w_ref[:, cols(direction)])
        yield
        return f_pre, v[bi, s, g], None

    streams = []
    for bi in range(n_batch):
        for direction, st_ref in enumerate((sf_ref, sb_ref)):
            st = st_ref.at[bi]
            st[...] = jnp.zeros(st.shape, F32)
            order = subtiles[::-1] if direction else subtiles
            streams.append(_hgrn_items(
                direction, order, lambda s, g, dr=direction, bi=bi: fetch(dr, bi, s, g), lg_ref,
                (cum_ref.at[direction], keep_ref.at[direction]), st, None))
    _interleave(*streams)


def _ctx_states(ctx, mods, ctx_row, norm_w, w_in, lb_logits, cum, keep):
    bsz, tc, d = ctx.shape
    dh = d // HEADS
    const2 = lambda b: (0, 0)
    nb = CTX_BATCH_PER_STEP
    state_spec = pl.BlockSpec((nb, HEADS, dh, dh), lambda b: (b, 0, 0, 0))
    state_shape = jax.ShapeDtypeStruct((bsz, HEADS, dh, dh), F32)
    return pl.pallas_call(
        _ctx_body,
        grid=(bsz // nb,),
        in_specs=[
            pl.BlockSpec((nb, tc, d), lambda b: (b, 0, 0)),
            _mod_spec(mods, 0, 0, ctx_row), _mod_spec(mods, 0, 1, ctx_row),
            pl.BlockSpec((1, d), const2),
            pl.BlockSpec((d, 3 * d), const2),
            pl.BlockSpec(lb_logits.shape, lambda b: (0, 0, 0, 0)),
            pl.BlockSpec(cum.shape, lambda b: (0, 0, 0)),
            pl.BlockSpec(keep.shape, lambda b: (0, 0, 0)),
        ],
        out_specs=[state_spec, state_spec],
        out_shape=[state_shape, state_shape],
        scratch_shapes=[pltpu.VMEM((nb, tc, d), BF16)],
        compiler_params=_COMPILER_PARAMS,
        name="ctx_states",
    )(ctx, mods, mods, norm_w, w_in, lb_logits, cum, keep)


def _load_initial_state(s0_ref, st_ref):
    @pl.when(pl.program_id(1) == 0)
    def _():
        st_ref[...] = s0_ref[0]


def _fwd_body(x_ref, shift_ref, scale_ref, nw_ref, w_ref, lg_ref, cum_ref, keep_ref, s0_ref,
              of_ref, fb_ref, v_ref, q_ref, zg_ref, st_ref, hb_ref):
    d = x_ref.shape[-1]
    dh = d // HEADS
    gw = HEADS_PER_GROUP * dh
    _load_initial_state(s0_ref, st_ref)
    normalised = set()

    def fetch(s, g):
        rows = _subtile_rows(s, FWD_SUBTILE)
        if s not in normalised:
            hb_ref[rows] = _normalised_rows(x_ref[0, rows], shift_ref, scale_ref, nw_ref)
            normalised.add(s)
        lanes = slice(g * gw, (g + 1) * gw)
        project = lambda block: _dot(
            hb_ref[rows], w_ref[:, block * d + g * gw:block * d + (g + 1) * gw])
        f_fwd = project(0)
        yield
        v = project(2).astype(BF16)
        v_ref[0, rows, lanes] = v
        yield
        q = _silu(project(3)) * (dh ** -0.5)
        q_ref[0, rows, lanes] = q.astype(BF16)
        yield
        fb_ref[0, rows, lanes] = project(1).astype(BF16)
        yield
        zg_ref[0, rows, lanes] = _silu(project(4)).astype(BF16)
        yield
        return f_fwd, v, q

    def emit(s, h, o):
        of_ref[0, _subtile_rows(s, FWD_SUBTILE), h * dh:(h + 1) * dh] = o.astype(BF16)

    subtiles = list(range(x_ref.shape[1] // FWD_SUBTILE))
    _interleave(_hgrn_items(0, subtiles, fetch, lg_ref, (cum_ref.at[0], keep_ref.at[0]), st_ref,
                            emit))


def _fwd_sweep(x, mods, norm_w, w_in, lb_logits, cum, keep, s0_f):
    bsz, t, d = x.shape
    dh = d // HEADS
    tq = FWD_TOKENS_PER_STEP
    tile = pl.BlockSpec((1, tq, d), lambda b, i: (b, i, 0))
    saved = jax.ShapeDtypeStruct((bsz, t, d), BF16)
    return pl.pallas_call(
        _fwd_body,
        grid=(bsz, t // tq),
        in_specs=[
            tile, _mod_spec(mods, 0, 0), _mod_spec(mods, 0, 1),
            pl.BlockSpec((1, d), lambda b, i: (0, 0)),
            _resident(w_in), _resident(lb_logits), _resident(cum), _resident(keep),
            pl.BlockSpec((1, HEADS, dh, dh), lambda b, i: (b, 0, 0, 0)),
        ],
        out_specs=[tile] * 5,
        out_shape=[saved] * 5,
        scratch_shapes=[pltpu.VMEM((HEADS, dh, dh), F32), pltpu.VMEM((tq, d), BF16)],
        compiler_params=_COMPILER_PARAMS,
        name="hgrn_fwd_sweep",
    )(x, mods, mods, norm_w, w_in, lb_logits, cum, keep, s0_f)


def _pool_stream(x1, shift_ref, scale_ref, gate_ref, nw_ref, wi_ref, band_ref, inv_ref, wg_ref,
                 ps_ref, wo_ref, fw_ref, hb_ref, act_ref, store):
    d = x1.shape[-1]
    gd = d // len(POOL_WINDOWS)
    hb_ref[...] = _normalised_rows(x1, shift_ref, scale_ref, nw_ref)

    def issue(g):
        u = _dot(hb_ref[...], wi_ref[:, g * gd:(g + 1) * gd])
        yield
        z = _dot(hb_ref[...], wi_ref[:, d + g * gd:d + (g + 1) * gd])
        yield
        return u, z

    def prepare(g, projected):
        u, z = projected
        return u, z, u.astype(BF16)

    def first(g, prepared):
        u, z, u_bf = prepared
        band = band_ref[g]
        mean = _by_mask_rows(lambda rows: _dot(band, rows) * inv_ref[g], u_bf)
        yield
        return g, u, z, mean

    def second(cont):
        g, u, z, mean = cont
        pooled = mean - u
        y = _dot(pooled.astype(BF16), wg_ref[g]) * ps_ref[:, g * gd:(g + 1) * gd]
        act_ref[:, g * gd:(g + 1) * gd] = (y * _silu(z)).astype(BF16)
        yield

    def finish_after(g):
        if g != len(POOL_WINDOWS) - 1:
            return None

        def finish():
            mix = _dot(act_ref[...], wo_ref[...])
            store(_rmsnorm(x1 + gate_ref[0] * mix, fw_ref[...]))

        return finish

    return _pipeline(list(range(len(POOL_WINDOWS))), issue, prepare, first, second, finish_after)


def _bwd_body(x_ref, of_ref, fb_ref, v_ref, q_ref, zg_ref, gate0_ref, lg_ref, cum_ref, keep_ref,
              s0_ref, gw_ref, wo_ref, shift1_ref, scale1_ref, gate1_ref, nw1_ref, wi_ref,
              band_ref, inv_ref, wg_ref, ps_ref, wo1_ref, fw_ref, out_ref, st_ref, act0_ref,
              hb1_ref, act1_ref):
    d = x_ref.shape[-1]
    dh = d // HEADS
    gw = HEADS_PER_GROUP * dh
    _load_initial_state(s0_ref, st_ref)
    masks = (cum_ref.at[1], keep_ref.at[1])

    def fetch(s, g):
        at = (0, _subtile_rows(s, BWD_SUBTILE), slice(g * gw, (g + 1) * gw))
        yield
        return fb_ref[at].astype(F32), v_ref[at], q_ref[at].astype(F32)

    def emit(s, h, o):
        at = (0, _subtile_rows(s, BWD_SUBTILE), slice(h * dh, (h + 1) * dh))
        o = _rmsnorm(of_ref[at].astype(F32) + o, gw_ref[:, at[2]])
        act0_ref[at[1:]] = (o * zg_ref[at].astype(F32)).astype(BF16)

    def all_layers(s):
        yield from _hgrn_items(1, [s], fetch, lg_ref, masks, st_ref, emit, order)
        rows = _subtile_rows(s, BWD_SUBTILE)
        x1 = x_ref[0, rows] + gate0_ref[0] * _dot(act0_ref[rows], wo_ref[...])
        yield

        def store(out):
            out_ref[0, rows] = out

        yield from _pool_stream(x1, shift1_ref, scale1_ref, gate1_ref, nw1_ref, wi_ref, band_ref,
                                inv_ref, wg_ref, ps_ref, wo1_ref, fw_ref, hb1_ref.at[rows],
                                act1_ref.at[rows], store)

    subtiles = list(range(x_ref.shape[1] // BWD_SUBTILE))[::-1]
    order = _StateOrder(subtiles)
    _interleave(*[all_layers(s) for s in subtiles], lag=STREAM_LAG)


def _bwd_sweep(x, saved, mods, lb_logits, cum, keep, s0_b, gnorm_w, w_out, norm_w1, pool_w_in,
               band, inv_count, w_grp, pool_scale, pool_w_out, final_w):
    bsz, t, d = x.shape
    dh = d // HEADS
    tq = BWD_TOKENS_PER_STEP
    nt = t // tq
    tile = pl.BlockSpec((1, tq, d), lambda b, i: (b, nt - 1 - i, 0))
    vec = pl.BlockSpec((1, d), lambda b, i: (0, 0))
    whole = _resident
    return pl.pallas_call(
        _bwd_body,
        grid=(bsz, nt),
        in_specs=[tile] * 6 + [
            _mod_spec(mods, 0, 2), whole(lb_logits), whole(cum), whole(keep),
            pl.BlockSpec((1, HEADS, dh, dh), lambda b, i: (b, 0, 0, 0)),
            vec, whole(w_out), _mod_spec(mods, 1, 0), _mod_spec(mods, 1, 1), _mod_spec(mods, 1, 2),
            vec, whole(pool_w_in), whole(band),
            whole(inv_count), whole(w_grp), vec, whole(pool_w_out), vec,
        ],
        out_specs=tile,
        out_shape=jax.ShapeDtypeStruct((bsz, t, d), F32),
        scratch_shapes=[pltpu.VMEM((HEADS, dh, dh), F32)] + [pltpu.VMEM((tq, d), BF16)] * 3,
        compiler_params=_COMPILER_PARAMS,
        name="hgrn_bwd_pool_sweep",
    )(x, *saved, mods, lb_logits, cum, keep, s0_b, gnorm_w, w_out, mods, mods, mods,
      norm_w1, pool_w_in, band, inv_count, w_grp, pool_scale, pool_w_out, final_w)


def _cumulative_matrices(n):
    r = jnp.arange(n)[:, None]
    c = jnp.arange(n)[None, :]
    same = (r // CHUNK) == (c // CHUNK)
    cum = jnp.stack([same & (c <= r), same & (c >= r)])
    return cum.astype(BF16), cum[:, :INTRA_ROWS, :INTRA_ROWS].astype(F32)


def _pool_windows(n):
    r = jnp.arange(n)[:, None]
    c = jnp.arange(n)[None, :]
    same = (r // GRID_W) == (c // GRID_W)
    band = jnp.stack([(same & (c >= r - w // 2) & (c < r - w // 2 + w)) for w in POOL_WINDOWS])
    count = jnp.sum(band.astype(F32), axis=-1, keepdims=True)
    return band.astype(BF16), 1.0 / count


def kernel(x, c, ctx, c_ctx, ada_w, ada_b, norm_w, hgrn_w_in, hgrn_lb_logits, hgrn_gnorm_w,
           hgrn_w_out, pool_w_in, pool_w_grp, pool_scale, pool_w_out, final_norm_w):
    bsz, t, d = x.shape
    depth = ada_w.shape[0]
    assert depth == 2 and hgrn_w_in.shape[0] == 1 and pool_w_in.shape[0] == 1
    assert t % FWD_TOKENS_PER_STEP == 0 and t % BWD_TOKENS_PER_STEP == 0
    assert ctx.shape[1] % MASK_ROWS == 0 and bsz < COND_ROWS
    assert bsz % CTX_BATCH_PER_STEP == 0 and ada_w.shape[2] == 3 * d
    assert FWD_TOKENS_PER_STEP % FWD_SUBTILE == 0 and BWD_TOKENS_PER_STEP % BWD_SUBTILE == 0
    assert FWD_SUBTILE % MASK_ROWS == 0 and BWD_SUBTILE % MASK_ROWS == 0
    assert MASK_ROWS % INTRA_ROWS == 0 and INTRA_ROWS % CHUNK == 0 and MASK_ROWS % GRID_W == 0

    cond = jnp.zeros((COND_ROWS, d), F32).at[:bsz].set(c).at[bsz].set(c_ctx)
    mods = _ada_mods(cond, ada_w, ada_b).reshape(depth * 3 * COND_ROWS, 1, d)

    w_in = hgrn_w_in[0].astype(BF16)
    lb_logits = hgrn_lb_logits.reshape(hgrn_lb_logits.shape[0], 2, 1, d)
    cum, keep = _cumulative_matrices(MASK_ROWS)
    band, inv_count = _pool_windows(MASK_ROWS)

    s0_f, s0_b = _ctx_states(ctx, mods, bsz, norm_w[0:1], w_in, lb_logits, cum, keep)
    saved = _fwd_sweep(x, mods, norm_w[0:1], w_in, lb_logits, cum, keep, s0_f)
    return _bwd_sweep(x, saved, mods, lb_logits, cum, keep, s0_b, hgrn_gnorm_w[0:1],
                      hgrn_w_out[0].astype(BF16), norm_w[1:2], pool_w_in[0].astype(BF16), band,
                      inv_count, pool_w_grp[0].astype(BF16), pool_scale[0:1],
                      pool_w_out[0].astype(BF16), final_norm_w.reshape(1, d))
```

```python
import jax
import jax.numpy as jnp
from jax import lax
from jax.experimental import pallas as pl
from jax.experimental.pallas import tpu as pltpu

HEADS = 8
CHUNK = 64
GRID_W = 64
POOL_WINDOWS = (2, 4, 8, 16)
EPS = 1e-6
MASK_ROWS = 256
FWD_SUBTILE = 512
BWD_SUBTILE = 256
FWD_TOKENS_PER_STEP = 1024
BWD_TOKENS_PER_STEP = 1024
HEADS_PER_GROUP = 2
INTRA_ROWS = 128
LOOKAHEAD = 2
RHS_READY = (False, True)
STREAM_LAG = 20
CTX_BATCH_PER_STEP = 4
COND_ROWS = 16
VMEM_LIMIT_BYTES = 60 * 1024 * 1024

F32 = jnp.float32
BF16 = jnp.bfloat16

_COMPILER_PARAMS = pltpu.CompilerParams(vmem_limit_bytes=VMEM_LIMIT_BYTES)


def _silu(a):
    half = 0.5 * a
    return half + half * jnp.tanh(half)


def _rmsnorm(xf, w):
    y = xf * lax.rsqrt(jnp.mean(xf * xf, axis=-1, keepdims=True) + EPS)
    return y * w


def _normalised_rows(x, shift_ref, scale_ref, nw_ref):
    gain = nw_ref[...] * (1.0 + scale_ref[0])
    return (_rmsnorm(x, gain) + shift_ref[0]).astype(BF16)


def _dot(a, b):
    return jnp.dot(a, b, preferred_element_type=F32)


def _dot_nt(a, b):
    return lax.dot_general(a, b, (((1,), (1,)), ((), ())), preferred_element_type=F32)


def _dot_tn(a, b):
    return lax.dot_general(a, b, (((0,), (0,)), ((), ())), preferred_element_type=F32)


def _lower_bound(logits, layer):
    rows = [logits[n] for n in range(logits.shape[0])]
    m = jnp.zeros_like(rows[0])
    for r in rows:
        m = jnp.maximum(m, r)
    exps = [jnp.exp(r - m) for r in rows]
    denom = jnp.exp(-m)
    for e in exps:
        denom = denom + e
    acc = exps[0]
    for e in exps[1:layer + 1]:
        acc = acc + e
    return acc / denom


def _subtile_rows(s, size):
    return slice(s * size, (s + 1) * size)


def _by_mask_rows(fn, *arrays):
    n = arrays[0].shape[0] // MASK_ROWS
    blocks = [fn(*(a[i * MASK_ROWS:(i + 1) * MASK_ROWS] for a in arrays)) for i in range(n)]
    return blocks[0] if n == 1 else jnp.concatenate(blocks, axis=0)


def _resident(a):
    return pl.BlockSpec(a.shape, lambda b, i: (0,) * a.ndim, pipeline_mode=pl.Buffered(1))


class _Steps:
    def __init__(self, gen):
        self._gen = gen
        self.done = False
        self.value = None

    def step(self):
        if not self.done:
            try:
                next(self._gen)
            except StopIteration as stop:
                self.done = True
                self.value = stop.value
        return not self.done


def _alongside(main, side):
    while main.step():
        yield
        if side.step():
            yield
    return main.value


def _pipeline(items, issue, prepare, first, second, finish_after):
    ahead = min(LOOKAHEAD, len(items))
    issued = []
    for item in items[:ahead]:
        issued.append((yield from issue(item)))
    prepared = []
    for item, handle in zip(items, issued):
        prepared.append(prepare(item, handle))
        yield
    deferred = []
    for i, item in enumerate(items):
        later = items[i + ahead] if i + ahead < len(items) else None
        issuing = _Steps(issue(later) if later is not None else iter(()))
        cont = yield from _alongside(_Steps(first(item, prepared[i])), issuing)
        for fn in deferred:
            fn()
            yield
        deferred = []
        yield from _alongside(_Steps(second(cont)), issuing)
        while issuing.step():
            yield
        fn = finish_after(item)
        if fn is not None:
            deferred.append(fn)
        if later is not None:
            prepared.append(prepare(later, issuing.value))
            yield
    for fn in deferred:
        fn()
        yield


def _interleave(*streams, lag=0):
    done = [False] * len(streams)
    rnd = 0
    while not all(done):
        for k, stream in enumerate(streams):
            if done[k] or rnd < k * lag:
                continue
            try:
                next(stream)
            except StopIteration:
                done[k] = True
        rnd += 1


def _head_operands(reverse, b, k, q, rhs_ready):
    nc = b.shape[0] // CHUNK
    mid = CHUNK // 2 if reverse else CHUNK // 2 - 1
    last = 0 if reverse else CHUNK - 1
    k_dec, q_dec, k_end, q_in, decay = [], [], [], [], []
    for n in range(nc):
        rows = slice(n * CHUNK, (n + 1) * CHUNK)
        bn = b[rows]
        b_mid = bn[mid:mid + 1]
        b_last = bn[last:last + 1]
        kd = k[rows] * jnp.exp2(b_mid - bn)
        k_dec.append(kd.T.astype(BF16) if rhs_ready[0] else kd.astype(BF16))
        k_end.append((kd * jnp.exp2(b_last - b_mid)).astype(BF16))
        decay.append(jnp.exp2(b_last))
        if q is not None:
            qd = q[rows] * jnp.exp2(bn - b_mid)
            q_dec.append(qd.astype(BF16))
            q_in.append((qd * jnp.exp2(b_mid)).astype(BF16))
    return k_dec, q_dec, k_end, q_in, decay


def _scan_head(reverse, operands, v, keep, st, rhs_ready):
    k_dec, q_dec, k_end, q_in, decay = operands
    nc = len(decay)
    per_block = INTRA_ROWS // CHUNK
    scores = []
    if q_dec:
        for j in range(nc // per_block):
            blk = slice(j * per_block, (j + 1) * per_block)
            q_blk = jnp.concatenate(q_dec[blk], axis=0)
            if rhs_ready[0]:
                scores.append(_dot(q_blk, jnp.concatenate(k_dec[blk], axis=1)))
            else:
                scores.append(_dot_nt(q_blk, jnp.concatenate(k_dec[blk], axis=0)))
        yield
    update = [_dot_tn(v[n * CHUNK:(n + 1) * CHUNK], k_end[n]) for n in range(nc)]
    yield
    if q_dec:
        scores = [jnp.where(keep, a, 0.0).astype(BF16) for a in scores]
        o_intra = [_dot(a, v[j * INTRA_ROWS:(j + 1) * INTRA_ROWS]) for j, a in enumerate(scores)]
        yield
    carried = [None] * nc
    for n in (range(nc - 1, -1, -1) if reverse else range(nc)):
        carried[n] = st.T.astype(BF16) if rhs_ready[1] else st.astype(BF16)
        st = st * decay[n] + update[n]
    if not q_dec:
        return None, st
    o_inter = [(_dot if rhs_ready[1] else _dot_nt)(q_in[n], carried[n]) for n in range(nc)]
    return jnp.concatenate(o_intra, axis=0) + jnp.concatenate(o_inter, axis=0), st


class _StateOrder:
    def __init__(self, subtiles):
        self._before = {s: subtiles[:i] for i, s in enumerate(subtiles)}
        self._written = set()

    def read(self, s, h):
        missing = [p for p in self._before[s] if (p, h) not in self._written]
        assert not missing, f"state of head {h} read for subtile {s} before {missing} wrote it"

    def wrote(self, s, h):
        self._written.add((s, h))


def _hgrn_items(direction, subtiles, fetch, lg_ref, masks, st_ref, emit, order=None,
                rhs_ready=RHS_READY):
    dh = st_ref.shape[-1]
    gw = HEADS_PER_GROUP * dh
    reverse = direction == 1
    cum_ref, keep_ref = masks
    cum = cum_ref[...]
    keep = keep_ref[...] != 0.0
    lb_all = _lower_bound(lg_ref[:, direction], 0)

    def issue(item):
        return (yield from fetch(*item))

    def prepare(item, fetched):
        s, g = item
        f_pre, v, q = fetched
        lb = lb_all[:, g * gw:(g + 1) * gw]
        f = 0.5 * (1.0 + lb) + (0.5 * (1.0 - lb)) * jnp.tanh(0.5 * f_pre)
        b = _by_mask_rows(lambda logf: _dot(cum, logf), jnp.log2(f).astype(BF16))
        k = 1.0 - f
        operands = []
        for hh in range(HEADS_PER_GROUP):
            lanes = slice(hh * dh, (hh + 1) * dh)
            operands.append(_head_operands(reverse, b[:, lanes], k[:, lanes],
                                           None if q is None else q[:, lanes], rhs_ready))
        return operands, v, q is not None

    def first(item, prepared):
        s, g = item
        operands, v, has_q = prepared
        heads = []
        for hh in range(HEADS_PER_GROUP):
            lanes = slice(hh * dh, (hh + 1) * dh)
            if order is not None:
                order.read(s, g * HEADS_PER_GROUP + hh)
            head = _Steps(_scan_head(reverse, operands[hh], v[:, lanes], keep,
                                     st_ref[g * HEADS_PER_GROUP + hh], rhs_ready))
            for _ in range(2 if has_q else 1):
                head.step()
                yield
            heads.append(head)
        return item, heads

    def second(cont):
        (s, g), heads = cont
        for hh, head in enumerate(heads):
            h = g * HEADS_PER_GROUP + hh
            while head.step():
                yield
            o, st = head.value
            st_ref[h] = st
            if order is not None:
                order.wrote(s, h)
            if emit is not None:
                emit(s, h, o)
            yield

    items = [(s, g) for s in subtiles for g in range(HEADS // HEADS_PER_GROUP)]
    return _pipeline(items, issue, prepare, first, second, lambda item: None)


def _ada_body(cond_ref, w_ref, b_ref, o_ref):
    o_ref[0, 0] = _dot(_silu(cond_ref[...]), w_ref[0]) + b_ref[0]


def _ada_mods(cond, ada_w, ada_b):
    depth, d, n = ada_w.shape
    return pl.pallas_call(
        _ada_body,
        grid=(depth, n // d),
        in_specs=[
            pl.BlockSpec((COND_ROWS, d), lambda l, j: (0, 0)),
            pl.BlockSpec((1, d, d), lambda l, j: (l, 0, j)),
            pl.BlockSpec((1, 1, d), lambda l, j: (l, 0, j)),
        ],
        out_specs=pl.BlockSpec((1, 1, COND_ROWS, d), lambda l, j: (l, j, 0, 0)),
        out_shape=jax.ShapeDtypeStruct((depth, n // d, COND_ROWS, d), F32),
        compiler_params=_COMPILER_PARAMS,
        name="ada_mods",
    )(cond, ada_w, ada_b.reshape(depth, 1, n))


def _mod_spec(mods, layer, which, row=None):
    base = (layer * 3 + which) * COND_ROWS
    if row is None:
        return pl.BlockSpec((1, 1, mods.shape[-1]), lambda b, *_: (base + b, 0, 0))
    return pl.BlockSpec((1, 1, mods.shape[-1]), lambda *_: (base + row, 0, 0))


def _ctx_body(ctx_ref, shift_ref, scale_ref, nw_ref, w_ref, lg_ref, cum_ref, keep_ref,
              sf_ref, sb_ref, hb_ref):
    n_batch, tc, d = ctx_ref.shape
    gw = HEADS_PER_GROUP * (d // HEADS)
    subtiles = list(range(tc // MASK_ROWS))
    normalised, v = set(), {}

    def fetch(direction, bi, s, g):
        rows = _subtile_rows(s, MASK_ROWS)
        hb = hb_ref.at[bi]
        if (bi, s) not in normalised:
            hb[rows] = _normalised_rows(ctx_ref[bi, rows], shift_ref, scale_ref, nw_ref)
            normalised.add((bi, s))
        cols = lambda block: slice(block * d + g * gw, block * d + (g + 1) * gw)
        if (bi, s, g) not in v:
            v[bi, s, g] = _dot(hb[rows], w_ref[:, cols(2)]).astype(BF16)
            yield
        f_pre = _dot(hb[rows], w_ref[:, cols(direction)])
        yield
        return f_pre, v[bi, s, g], None

    streams = []
    for bi in range(n_batch):
        for direction, st_ref in enumerate((sf_ref, sb_ref)):
            st = st_ref.at[bi]
            st[...] = jnp.zeros(st.shape, F32)
            order = subtiles[::-1] if direction else subtiles
            streams.append(_hgrn_items(
                direction, order, lambda s, g, dr=direction, bi=bi: fetch(dr, bi, s, g), lg_ref,
                (cum_ref.at[direction], keep_ref.at[direction]), st, None))
    _interleave(*streams)


def _ctx_states(ctx, mods, ctx_row, norm_w, w_in, lb_logits, cum, keep):
    bsz, tc, d = ctx.shape
    dh = d // HEADS
    const2 = lambda b: (0, 0)
    nb = CTX_BATCH_PER_STEP
    state_spec = pl.BlockSpec((nb, HEADS, dh, dh), lambda b: (b, 0, 0, 0))
    state_shape = jax.ShapeDtypeStruct((bsz, HEADS, dh, dh), F32)
    return pl.pallas_call(
        _ctx_body,
        grid=(bsz // nb,),
        in_specs=[
            pl.BlockSpec((nb, tc, d), lambda b: (b, 0, 0)),
            _mod_spec(mods, 0, 0, ctx_row), _mod_spec(mods, 0, 1, ctx_row),
            pl.BlockSpec((1, d), const2),
            pl.BlockSpec((d, 3 * d), const2),
            pl.BlockSpec(lb_logits.shape, lambda b: (0, 0, 0, 0)),
            pl.BlockSpec(cum.shape, lambda b: (0, 0, 0)),
            pl.BlockSpec(keep.shape, lambda b: (0, 0, 0)),
        ],
        out_specs=[state_spec, state_spec],
        out_shape=[state_shape, state_shape],
        scratch_shapes=[pltpu.VMEM((nb, tc, d), BF16)],
        compiler_params=_COMPILER_PARAMS,
        name="ctx_states",
    )(ctx, mods, mods, norm_w, w_in, lb_logits, cum, keep)


def _load_initial_state(s0_ref, st_ref):
    @pl.when(pl.program_id(1) == 0)
    def _():
        st_ref[...] = s0_ref[0]


def _fwd_body(x_ref, shift_ref, scale_ref, nw_ref, w_ref, lg_ref, cum_ref, keep_ref, s0_ref,
              of_ref, fb_ref, v_ref, q_ref, zg_ref, st_ref, hb_ref):
    d = x_ref.shape[-1]
    dh = d // HEADS
    gw = HEADS_PER_GROUP * dh
    _load_initial_state(s0_ref, st_ref)
    normalised = set()

    def fetch(s, g):
        rows = _subtile_rows(s, FWD_SUBTILE)
        if s not in normalised:
            hb_ref[rows] = _normalised_rows(x_ref[0, rows], shift_ref, scale_ref, nw_ref)
            normalised.add(s)
        lanes = slice(g * gw, (g + 1) * gw)
        project = lambda block: _dot(
            hb_ref[rows], w_ref[:, block * d + g * gw:block * d + (g + 1) * gw])
        f_fwd = project(0)
        yield
        v = project(2).astype(BF16)
        v_ref[0, rows, lanes] = v
        yield
        q = _silu(project(3)) * (dh ** -0.5)
        q_ref[0, rows, lanes] = q.astype(BF16)
        yield
        fb_ref[0, rows, lanes] = project(1).astype(BF16)
        yield
        zg_ref[0, rows, lanes] = _silu(project(4)).astype(BF16)
        yield
        return f_fwd, v, q

    def emit(s, h, o):
        of_ref[0, _subtile_rows(s, FWD_SUBTILE), h * dh:(h + 1) * dh] = o.astype(BF16)

    subtiles = list(range(x_ref.shape[1] // FWD_SUBTILE))
    _interleave(_hgrn_items(0, subtiles, fetch, lg_ref, (cum_ref.at[0], keep_ref.at[0]), st_ref,
                            emit))


def _fwd_sweep(x, mods, norm_w, w_in, lb_logits, cum, keep, s0_f):
    bsz, t, d = x.shape
    dh = d // HEADS
    tq = FWD_TOKENS_PER_STEP
    tile = pl.BlockSpec((1, tq, d), lambda b, i: (b, i, 0))
    saved = jax.ShapeDtypeStruct((bsz, t, d), BF16)
    return pl.pallas_call(
        _fwd_body,
        grid=(bsz, t // tq),
        in_specs=[
            tile, _mod_spec(mods, 0, 0), _mod_spec(mods, 0, 1),
            pl.BlockSpec((1, d), lambda b, i: (0, 0)),
            _resident(w_in), _resident(lb_logits), _resident(cum), _resident(keep),
            pl.BlockSpec((1, HEADS, dh, dh), lambda b, i: (b, 0, 0, 0)),
        ],
        out_specs=[tile] * 5,
        out_shape=[saved] * 5,
        scratch_shapes=[pltpu.VMEM((HEADS, dh, dh), F32), pltpu.VMEM((tq, d), BF16)],
        compiler_params=_COMPILER_PARAMS,
        name="hgrn_fwd_sweep",
    )(x, mods, mods, norm_w, w_in, lb_logits, cum, keep, s0_f)


def _pool_stream(x1, shift_ref, scale_ref, gate_ref, nw_ref, wi_ref, band_ref, inv_ref, wg_ref,
                 ps_ref, wo_ref, fw_ref, hb_ref, act_ref, store):
    d = x1.shape[-1]
    gd = d // len(POOL_WINDOWS)
    hb_ref[...] = _normalised_rows(x1, shift_ref, scale_ref, nw_ref)

    def issue(g):
        u = _dot(hb_ref[...], wi_ref[:, g * gd:(g + 1) * gd])
        yield
        z = _dot(hb_ref[...], wi_ref[:, d + g * gd:d + (g + 1) * gd])
        yield
        return u, z

    def prepare(g, projected):
        u, z = projected
        return u, z, u.astype(BF16)

    def first(g, prepared):
        u, z, u_bf = prepared
        band = band_ref[g]
        mean = _by_mask_rows(lambda rows: _dot(band, rows) * inv_ref[g], u_bf)
        yield
        return g, u, z, mean

    def second(cont):
        g, u, z, mean = cont
        pooled = mean - u
        y = _dot(pooled.astype(BF16), wg_ref[g]) * ps_ref[:, g * gd:(g + 1) * gd]
        act_ref[:, g * gd:(g + 1) * gd] = (y * _silu(z)).astype(BF16)
        yield

    def finish_after(g):
        if g != len(POOL_WINDOWS) - 1:
            return None

        def finish():
            mix = _dot(act_ref[...], wo_ref[...])
            store(_rmsnorm(x1 + gate_ref[0] * mix, fw_ref[...]))

        return finish

    return _pipeline(list(range(len(POOL_WINDOWS))), issue, prepare, first, second, finish_after)


def _bwd_body(x_ref, of_ref, fb_ref, v_ref, q_ref, zg_ref, gate0_ref, lg_ref, cum_ref, keep_ref,
              s0_ref, gw_ref, wo_ref, shift1_ref, scale1_ref, gate1_ref, nw1_ref, wi_ref,
              band_ref, inv_ref, wg_ref, ps_ref, wo1_ref, fw_ref, out_ref, st_ref, act0_ref,
              hb1_ref, act1_ref):
    d = x_ref.shape[-1]
    dh = d // HEADS
    gw = HEADS_PER_GROUP * dh
    _load_initial_state(s0_ref, st_ref)
    masks = (cum_ref.at[1], keep_ref.at[1])

    def fetch(s, g):
        at = (0, _subtile_rows(s, BWD_SUBTILE), slice(g * gw, (g + 1) * gw))
        yield
        return fb_ref[at].astype(F32), v_ref[at], q_ref[at].astype(F32)

    def emit(s, h, o):
        at = (0, _subtile_rows(s, BWD_SUBTILE), slice(h * dh, (h + 1) * dh))
        o = _rmsnorm(of_ref[at].astype(F32) + o, gw_ref[:, at[2]])
        act0_ref[at[1:]] = (o * zg_ref[at].astype(F32)).astype(BF16)

    def all_layers(s):
        yield from _hgrn_items(1, [s], fetch, lg_ref, masks, st_ref, emit, order)
        rows = _subtile_rows(s, BWD_SUBTILE)
        x1 = x_ref[0, rows] + gate0_ref[0] * _dot(act0_ref[rows], wo_ref[...])
        yield

        def store(out):
            out_ref[0, rows] = out

        yield from _pool_stream(x1, shift1_ref, scale1_ref, gate1_ref, nw1_ref, wi_ref, band_ref,
                                inv_ref, wg_ref, ps_ref, wo1_ref, fw_ref, hb1_ref.at[rows],
                                act1_ref.at[rows], store)

    subtiles = list(range(x_ref.shape[1] // BWD_SUBTILE))[::-1]
    order = _StateOrder(subtiles)
    _interleave(*[all_layers(s) for s in subtiles], lag=STREAM_LAG)


def _bwd_sweep(x, saved, mods, lb_logits, cum, keep, s0_b, gnorm_w, w_out, norm_w1, pool_w_in,
               band, inv_count, w_grp, pool_scale, pool_w_out, final_w):
    bsz, t, d = x.shape
    dh = d // HEADS
    tq = BWD_TOKENS_PER_STEP
    nt = t // tq
    tile = pl.BlockSpec((1, tq, d), lambda b, i: (b, nt - 1 - i, 0))
    vec = pl.BlockSpec((1, d), lambda b, i: (0, 0))
    whole = _resident
    return pl.pallas_call(
        _bwd_body,
        grid=(bsz, nt),
        in_specs=[tile] * 6 + [
            _mod_spec(mods, 0, 2), whole(lb_logits), whole(cum), whole(keep),
            pl.BlockSpec((1, HEADS, dh, dh), lambda b, i: (b, 0, 0, 0)),
            vec, whole(w_out), _mod_spec(mods, 1, 0), _mod_spec(mods, 1, 1), _mod_spec(mods, 1, 2),
            vec, whole(pool_w_in), whole(band),
            whole(inv_count), whole(w_grp), vec, whole(pool_w_out), vec,
        ],
        out_specs=tile,
        out_shape=jax.ShapeDtypeStruct((bsz, t, d), F32),
        scratch_shapes=[pltpu.VMEM((HEADS, dh, dh), F32)] + [pltpu.VMEM((tq, d), BF16)] * 3,
        compiler_params=_COMPILER_PARAMS,
        name="hgrn_bwd_pool_sweep",
    )(x, *saved, mods, lb_logits, cum, keep, s0_b, gnorm_w, w_out, mods, mods, mods,
      norm_w1, pool_w_in, band, inv_count, w_grp, pool_scale, pool_w_out, final_w)


def _cumulative_matrices(n):
    r = jnp.arange(n)[:, None]
    c = jnp.arange(n)[None, :]
    same = (r // CHUNK) == (c // CHUNK)
    cum = jnp.stack([same & (c <= r), same & (c >= r)])
    return cum.astype(BF16), cum[:, :INTRA_ROWS, :INTRA_ROWS].astype(F32)


def _pool_windows(n):
    r = jnp.arange(n)[:, None]
    c = jnp.arange(n)[None, :]
    same = (r // GRID_W) == (c // GRID_W)
    band = jnp.stack([(same & (c >= r - w // 2) & (c < r - w // 2 + w)) for w in POOL_WINDOWS])
    count = jnp.sum(band.astype(F32), axis=-1, keepdims=True)
    return band.astype(BF16), 1.0 / count


def kernel(x, c, ctx, c_ctx, ada_w, ada_b, norm_w, hgrn_w_in, hgrn_lb_logits, hgrn_gnorm_w,
           hgrn_w_out, pool_w_in, pool_w_grp, pool_scale, pool_w_out, final_norm_w):
    bsz, t, d = x.shape
    depth = ada_w.shape[0]
    assert depth == 2 and hgrn_w_in.shape[0] == 1 and pool_w_in.shape[0] == 1
    assert t % FWD_TOKENS_PER_STEP == 0 and t % BWD_TOKENS_PER_STEP == 0
    assert ctx.shape[1] % MASK_ROWS == 0 and bsz < COND_ROWS
    assert bsz % CTX_BATCH_PER_STEP == 0 and ada_w.shape[2] == 3 * d
    assert FWD_TOKENS_PER_STEP % FWD_SUBTILE == 0 and BWD_TOKENS_PER_STEP % BWD_SUBTILE == 0
    assert FWD_SUBTILE % MASK_ROWS == 0 and BWD_SUBTILE % MASK_ROWS == 0
    assert MASK_ROWS % INTRA_ROWS == 0 and INTRA_ROWS % CHUNK == 0 and MASK_ROWS % GRID_W == 0

    cond = jnp.zeros((COND_ROWS, d), F32).at[:bsz].set(c).at[bsz].set(c_ctx)
    mods = _ada_mods(cond, ada_w, ada_b).reshape(depth * 3 * COND_ROWS, 1, d)

    w_in = hgrn_w_in[0].astype(BF16)
    lb_logits = hgrn_lb_logits.reshape(hgrn_lb_logits.shape[0], 2, 1, d)
    cum, keep = _cumulative_matrices(MASK_ROWS)
    band, inv_count = _pool_windows(MASK_ROWS)

    s0_f, s0_b = _ctx_states(ctx, mods, bsz, norm_w[0:1], w_in, lb_logits, cum, keep)
    saved = _fwd_sweep(x, mods, norm_w[0:1], w_in, lb_logits, cum, keep, s0_f)
    return _bwd_sweep(x, saved, mods, lb_logits, cum, keep, s0_b, hgrn_gnorm_w[0:1],
                      hgrn_w_out[0].astype(BF16), norm_w[1:2], pool_w_in[0].astype(BF16), band,
                      inv_count, pool_w_grp[0].astype(BF16), pool_scale[0:1],
                      pool_w_out[0].astype(BF16), final_norm_w.reshape(1, d))
```

```python
import jax
import jax.numpy as jnp
from jax import lax
from jax.experimental import pallas as pl
from jax.experimental.pallas import tpu as pltpu

HEADS = 8
CHUNK = 64
GRID_W = 64
POOL_WINDOWS = (2, 4, 8, 16)
EPS = 1e-6
MXU_COLUMNS_V7X = 256
VMEM_BYTES_V7X = 64 * 1024 * 1024
MASK_ROWS = 256
FWD_SUBTILE = 512
BWD_SUBTILE = 256
FWD_TOKENS_PER_STEP = 1024
BWD_TOKENS_PER_STEP = 1024
HEADS_PER_GROUP = 2
INTRA_ROWS = 128
LOOKAHEAD = 2
FWD_RHS_READY = (True, True)
BWD_RHS_READY = (False, True)
STREAM_LAG = 20
CTX_BATCH_PER_STEP = 4
COND_ROWS = 16
VMEM_LIMIT_BYTES = VMEM_BYTES_V7X - 4 * 1024 * 1024

F32 = jnp.float32
BF16 = jnp.bfloat16

_COMPILER_PARAMS = pltpu.CompilerParams(vmem_limit_bytes=VMEM_LIMIT_BYTES)


def _silu(a):
    half = 0.5 * a
    return half + half * jnp.tanh(half)


def _rmsnorm(xf, w):
    y = xf * lax.rsqrt(jnp.mean(xf * xf, axis=-1, keepdims=True) + EPS)
    return y * w


def _normalised_rows(x, shift_ref, scale_ref, nw_ref):
    gain = nw_ref[...] * (1.0 + scale_ref[0])
    return (_rmsnorm(x, gain) + shift_ref[0]).astype(BF16)


def _dot(a, b):
    return jnp.dot(a, b, preferred_element_type=F32)


def _dot_nt(a, b):
    return lax.dot_general(a, b, (((1,), (1,)), ((), ())), preferred_element_type=F32)


def _dot_tn(a, b):
    return lax.dot_general(a, b, (((0,), (0,)), ((), ())), preferred_element_type=F32)


def _lower_bound(lg_ref, direction, layer):
    rows = [lg_ref[n, direction:direction + 1, :] for n in range(lg_ref.shape[0])]
    m = jnp.zeros_like(rows[0])
    for r in rows:
        m = jnp.maximum(m, r)
    exps = [jnp.exp(r - m) for r in rows]
    denom = jnp.exp(-m)
    for e in exps:
        denom = denom + e
    acc = exps[0]
    for e in exps[1:layer + 1]:
        acc = acc + e
    return acc / denom


def _subtile_rows(s, size):
    return slice(s * size, (s + 1) * size)


def _by_mask_rows(fn, *arrays):
    n = arrays[0].shape[0] // MASK_ROWS
    blocks = [fn(*(a[i * MASK_ROWS:(i + 1) * MASK_ROWS] for a in arrays)) for i in range(n)]
    return blocks[0] if n == 1 else jnp.concatenate(blocks, axis=0)


def _resident(a):
    return pl.BlockSpec(a.shape, lambda b, i: (0,) * a.ndim, pipeline_mode=pl.Buffered(1))


class _Steps:
    def __init__(self, gen):
        self._gen = gen
        self.done = False
        self.value = None

    def step(self):
        if not self.done:
            try:
                next(self._gen)
            except StopIteration as stop:
                self.done = True
                self.value = stop.value
        return not self.done


def _alongside(main, side):
    while main.step():
        yield
        if side.step():
            yield
    return main.value


def _pipeline(items, issue, prepare, first, second, finish_after):
    ahead = min(LOOKAHEAD, len(items))
    issued = []
    for item in items[:ahead]:
        issued.append((yield from issue(item)))
    prepared = []
    for item, handle in zip(items, issued):
        prepared.append(prepare(item, handle))
        yield
    deferred = []
    for i, item in enumerate(items):
        later = items[i + ahead] if i + ahead < len(items) else None
        issuing = _Steps(issue(later) if later is not None else iter(()))
        cont = yield from _alongside(_Steps(first(item, prepared[i])), issuing)
        for fn in deferred:
            fn()
            yield
        deferred = []
        yield from _alongside(_Steps(second(cont)), issuing)
        while issuing.step():
            yield
        fn = finish_after(item)
        if fn is not None:
            deferred.append(fn)
        if later is not None:
            prepared.append(prepare(later, issuing.value))
            yield
    for fn in deferred:
        fn()
        yield


def _interleave(*streams, lag=0):
    done = [False] * len(streams)
    rnd = 0
    while not all(done):
        for k, stream in enumerate(streams):
            if done[k] or rnd < k * lag:
                continue
            try:
                next(stream)
            except StopIteration:
                done[k] = True
        rnd += 1


def _head_operands(reverse, b, k, q, rhs_ready):
    nc = b.shape[0] // CHUNK
    mid = CHUNK // 2 if reverse else CHUNK // 2 - 1
    last = 0 if reverse else CHUNK - 1
    if q is None:
        b_last = [b[n * CHUNK + last:n * CHUNK + last + 1] for n in range(nc)]
        to_end, total = {}, jnp.zeros_like(b_last[0])
        for n in (range(nc) if reverse else range(nc - 1, -1, -1)):
            to_end[n] = total
            total = total + b_last[n]
        k_end = [k[n * CHUNK:(n + 1) * CHUNK]
                 * jnp.exp2(b_last[n] + to_end[n] - b[n * CHUNK:(n + 1) * CHUNK])
                 for n in range(nc)]
        return [], [], [jnp.concatenate(k_end, axis=0).astype(BF16)], [], [jnp.exp2(total)]
    k_dec, q_dec, k_end, q_in, decay = [], [], [], [], []
    for n in range(nc):
        rows = slice(n * CHUNK, (n + 1) * CHUNK)
        bn = b[rows]
        b_mid = bn[mid:mid + 1]
        b_last = bn[last:last + 1]
        kd = k[rows] * jnp.exp2(b_mid - bn)
        k_dec.append(kd if rhs_ready[0] else kd.astype(BF16))
        k_end.append((kd * jnp.exp2(b_last - b_mid)).astype(BF16))
        decay.append(jnp.exp2(b_last))
        qd = q[rows] * jnp.exp2(bn - b_mid)
        q_dec.append(qd.astype(BF16))
        q_in.append((qd * jnp.exp2(b_mid)).astype(BF16))
    if rhs_ready[0]:
        per_block = INTRA_ROWS // CHUNK
        k_dec = [jnp.concatenate(k_dec[j:j + per_block], axis=0).T.astype(BF16)
                 for j in range(0, nc, per_block)]
    return k_dec, q_dec, k_end, q_in, decay


def _scan_head(reverse, operands, v, keep, st, rhs_ready):
    k_dec, q_dec, k_end, q_in, decay = operands
    nc = len(decay)
    per_block = INTRA_ROWS // CHUNK
    scores = []
    if q_dec:
        for j in range(nc // per_block):
            blk = slice(j * per_block, (j + 1) * per_block)
            q_blk = jnp.concatenate(q_dec[blk], axis=0)
            if rhs_ready[0]:
                scores.append(_dot(q_blk, k_dec[j]))
            else:
                scores.append(_dot_nt(q_blk, jnp.concatenate(k_dec[blk], axis=0)))
        yield
    span = v.shape[0] // nc
    update = [_dot_tn(v[n * span:(n + 1) * span], k_end[n]) for n in range(nc)]
    yield
    if q_dec:
        scores = [jnp.where(keep, a, 0.0).astype(BF16) for a in scores]
        o_intra = [_dot(a, v[j * INTRA_ROWS:(j + 1) * INTRA_ROWS]) for j, a in enumerate(scores)]
        yield
    carried = [None] * nc
    for n in (range(nc - 1, -1, -1) if reverse else range(nc)):
        carried[n] = st.T.astype(BF16) if rhs_ready[1] else st.astype(BF16)
        st = st * decay[n] + update[n]
    if not q_dec:
        return None, st
    o_inter = [(_dot if rhs_ready[1] else _dot_nt)(q_in[n], carried[n]) for n in range(nc)]
    return jnp.concatenate(o_intra, axis=0) + jnp.concatenate(o_inter, axis=0), st


class _StateOrder:
    def __init__(self, subtiles):
        self._before = {s: subtiles[:i] for i, s in enumerate(subtiles)}
        self._written = set()

    def read(self, s, h):
        missing = [p for p in self._before[s] if (p, h) not in self._written]
        assert not missing, f"state of head {h} read for subtile {s} before {missing} wrote it"

    def wrote(self, s, h):
        self._written.add((s, h))


def _hgrn_items(direction, subtiles, fetch, lg_ref, masks, st_ref, emit, order=None,
                rhs_ready=FWD_RHS_READY):
    dh = st_ref.shape[-1]
    gw = HEADS_PER_GROUP * dh
    reverse = direction == 1
    cum_ref, keep_ref = masks
    cum = cum_ref[...]
    keep = keep_ref[...] != 0.0
    lb_all = _lower_bound(lg_ref, direction, 0)

    def issue(item):
        return (yield from fetch(*item))

    def prepare(item, fetched):
        s, g = item
        f_pre, v, q = fetched
        lb = lb_all[:, g * gw:(g + 1) * gw]
        f = 0.5 * (1.0 + lb) + (0.5 * (1.0 - lb)) * jnp.tanh(0.5 * f_pre)
        b = _by_mask_rows(lambda logf: _dot(cum, logf), jnp.log2(f).astype(BF16))
        k = 1.0 - f
        operands = []
        for hh in range(HEADS_PER_GROUP):
            lanes = slice(hh * dh, (hh + 1) * dh)
            operands.append(_head_operands(reverse, b[:, lanes], k[:, lanes],
                                           None if q is None else q[:, lanes], rhs_ready))
        return operands, v, q is not None

    def first(item, prepared):
        s, g = item
        operands, v, has_q = prepared
        heads = []
        for hh in range(HEADS_PER_GROUP):
            lanes = slice(hh * dh, (hh + 1) * dh)
            if order is not None:
                order.read(s, g * HEADS_PER_GROUP + hh)
            head = _Steps(_scan_head(reverse, operands[hh], v[:, lanes], keep,
                                     st_ref[g * HEADS_PER_GROUP + hh], rhs_ready))
            for _ in range(2 if has_q else 1):
                head.step()
                yield
            heads.append(head)
        return item, heads

    def second(cont):
        (s, g), heads = cont
        for hh, head in enumerate(heads):
            h = g * HEADS_PER_GROUP + hh
            while head.step():
                yield
            o, st = head.value
            st_ref[h] = st
            if order is not None:
                order.wrote(s, h)
            if emit is not None:
                emit(s, h, o)
            yield

    items = [(s, g) for s in subtiles for g in range(HEADS // HEADS_PER_GROUP)]
    return _pipeline(items, issue, prepare, first, second, lambda item: None)


def _ada_body(cond_ref, w_ref, b_ref, o_ref):
    o_ref[:, 0, :] = _dot(_silu(cond_ref[...]), w_ref[0]) + b_ref[0]


def _ada_mods(cond, ada_w, ada_b):
    depth, d, n = ada_w.shape
    return pl.pallas_call(
        _ada_body,
        grid=(depth, n // d),
        in_specs=[
            pl.BlockSpec((COND_ROWS, d), lambda l, j: (0, 0)),
            pl.BlockSpec((1, d, d), lambda l, j: (l, 0, j)),
            pl.BlockSpec((1, 1, d), lambda l, j: (l, 0, j)),
        ],
        out_specs=pl.BlockSpec((COND_ROWS, 1, d), lambda l, j: (l * (n // d) + j, 0, 0)),
        out_shape=jax.ShapeDtypeStruct((depth * (n // d) * COND_ROWS, 1, d), F32),
        compiler_params=_COMPILER_PARAMS,
        name="ada_mods",
    )(cond, ada_w, ada_b.reshape(depth, 1, n))


def _mod_spec(mods, layer, which, row=None):
    base = (layer * 3 + which) * COND_ROWS
    if row is None:
        return pl.BlockSpec((1, 1, mods.shape[-1]), lambda b, *_: (base + b, 0, 0))
    return pl.BlockSpec((1, 1, mods.shape[-1]), lambda *_: (base + row, 0, 0))


def _ctx_body(ctx_ref, shift_ref, scale_ref, nw_ref, w_ref, lg_ref, cum_ref, keep_ref,
              sf_ref, sb_ref, hb_ref):
    n_batch, tc, d = ctx_ref.shape
    gw = HEADS_PER_GROUP * (d // HEADS)
    subtiles = list(range(tc // MASK_ROWS))
    normalised, v = set(), {}

    def fetch(direction, bi, s, g):
        rows = _subtile_rows(s, MASK_ROWS)
        hb = hb_ref.at[bi]
        if (bi, s) not in normalised:
            hb[rows] = _normalised_rows(ctx_ref[bi, rows], shift_ref, scale_ref,
                                        nw_ref.at[0:1])
            normalised.add((bi, s))
        cols = lambda block: slice(block * d + g * gw, block * d + (g + 1) * gw)
        if (bi, s, g) not in v:
            v[bi, s, g] = _dot(hb[rows], w_ref[:, cols(2)]).astype(BF16)
            yield
        f_pre = _dot(hb[rows], w_ref[:, cols(direction)])
        yield
        return f_pre, v[bi, s, g], None

    streams = []
    for bi in range(n_batch):
        for direction, st_ref in enumerate((sf_ref, sb_ref)):
            st = st_ref.at[bi]
            st[...] = jnp.zeros(st.shape, F32)
            order = subtiles[::-1] if direction else subtiles
            streams.append(_hgrn_items(
                direction, order, lambda s, g, dr=direction, bi=bi: fetch(dr, bi, s, g), lg_ref,
                (cum_ref.at[direction], keep_ref.at[direction]), st, None))
    _interleave(*streams)


def _ctx_states(ctx, mods, ctx_row, norm_w, w_in, lb_logits, cum, keep):
    bsz, tc, d = ctx.shape
    dh = d // HEADS
    const2 = lambda b: (0, 0)
    nb = CTX_BATCH_PER_STEP
    state_spec = pl.BlockSpec((nb, HEADS, dh, dh), lambda b: (b, 0, 0, 0))
    state_shape = jax.ShapeDtypeStruct((bsz, HEADS, dh, dh), F32)
    return pl.pallas_call(
        _ctx_body,
        grid=(bsz // nb,),
        in_specs=[
            pl.BlockSpec((nb, tc, d), lambda b: (b, 0, 0)),
            _mod_spec(mods, 0, 0, ctx_row), _mod_spec(mods, 0, 1, ctx_row),
            pl.BlockSpec(norm_w.shape, const2),
            pl.BlockSpec((d, 3 * d), const2),
            pl.BlockSpec(lb_logits.shape, lambda b: (0, 0, 0)),
            pl.BlockSpec(cum.shape, lambda b: (0, 0, 0)),
            pl.BlockSpec(keep.shape, lambda b: (0, 0, 0)),
        ],
        out_specs=[state_spec, state_spec],
        out_shape=[state_shape, state_shape],
        scratch_shapes=[pltpu.VMEM((nb, tc, d), BF16)],
        compiler_params=_COMPILER_PARAMS,
        name="ctx_states",
    )(ctx, mods, mods, norm_w, w_in, lb_logits, cum, keep)


def _load_initial_state(s0_ref, st_ref):
    @pl.when(pl.program_id(1) == 0)
    def _():
        st_ref[...] = s0_ref[0]


def _fwd_body(x_ref, shift_ref, scale_ref, nw_ref, w_ref, lg_ref, cum_ref, keep_ref, s0_ref,
              of_ref, fb_ref, v_ref, q_ref, zg_ref, st_ref, hb_ref):
    d = x_ref.shape[-1]
    dh = d // HEADS
    gw = HEADS_PER_GROUP * dh
    _load_initial_state(s0_ref, st_ref)
    normalised = set()

    def fetch(s, g):
        rows = _subtile_rows(s, FWD_SUBTILE)
        if s not in normalised:
            hb_ref[rows] = _normalised_rows(x_ref[0, rows], shift_ref, scale_ref,
                                            nw_ref.at[0:1])
            normalised.add(s)
        lanes = slice(g * gw, (g + 1) * gw)
        project = lambda block: _dot(
            hb_ref[rows], w_ref[:, block * d + g * gw:block * d + (g + 1) * gw])
        f_fwd = project(0)
        yield
        v = project(2).astype(BF16)
        v_ref[0, rows, lanes] = v
        yield
        q = _silu(project(3)) * (dh ** -0.5)
        q_ref[0, rows, lanes] = q.astype(BF16)
        yield
        fb_ref[0, rows, lanes] = project(1).astype(BF16)
        yield
        zg_ref[0, rows, lanes] = _silu(project(4)).astype(BF16)
        yield
        return f_fwd, v, q

    def emit(s, h, o):
        of_ref[0, _subtile_rows(s, FWD_SUBTILE), h * dh:(h + 1) * dh] = o.astype(BF16)

    subtiles = list(range(x_ref.shape[1] // FWD_SUBTILE))
    _interleave(_hgrn_items(0, subtiles, fetch, lg_ref, (cum_ref.at[0], keep_ref.at[0]), st_ref,
                            emit))


def _fwd_sweep(x, mods, norm_w, w_in, lb_logits, cum, keep, s0_f):
    bsz, t, d = x.shape
    dh = d // HEADS
    tq = FWD_TOKENS_PER_STEP
    tile = pl.BlockSpec((1, tq, d), lambda b, i: (b, i, 0))
    saved = jax.ShapeDtypeStruct((bsz, t, d), BF16)
    return pl.pallas_call(
        _fwd_body,
        grid=(bsz, t // tq),
        in_specs=[
            tile, _mod_spec(mods, 0, 0), _mod_spec(mods, 0, 1), _resident(norm_w),
            _resident(w_in), _resident(lb_logits), _resident(cum), _resident(keep),
            pl.BlockSpec((1, HEADS, dh, dh), lambda b, i: (b, 0, 0, 0)),
        ],
        out_specs=[tile] * 5,
        out_shape=[saved] * 5,
        scratch_shapes=[pltpu.VMEM((HEADS, dh, dh), F32), pltpu.VMEM((tq, d), BF16)],
        compiler_params=_COMPILER_PARAMS,
        name="hgrn_fwd_sweep",
    )(x, mods, mods, norm_w, w_in, lb_logits, cum, keep, s0_f)


def _pool_stream(x1, shift_ref, scale_ref, gate_ref, nw_ref, wi_ref, band_ref, inv_ref, wg_ref,
                 ps_ref, wo_ref, fw_ref, hb_ref, act_ref, store):
    d = x1.shape[-1]
    gd = d // len(POOL_WINDOWS)
    hb_ref[...] = _normalised_rows(x1, shift_ref, scale_ref, nw_ref)

    def issue(g):
        u = _dot(hb_ref[...], wi_ref[:, g * gd:(g + 1) * gd])
        yield
        z = _dot(hb_ref[...], wi_ref[:, d + g * gd:d + (g + 1) * gd])
        yield
        return u, z

    def prepare(g, projected):
        u, z = projected
        return u, z, u.astype(BF16)

    def first(g, prepared):
        u, z, u_bf = prepared
        band = band_ref[g]
        mean = _by_mask_rows(lambda rows: _dot(band, rows) * inv_ref[g], u_bf)
        yield
        return g, u, z, mean

    def second(cont):
        g, u, z, mean = cont
        pooled = mean - u
        y = _dot(pooled.astype(BF16), wg_ref[g]) * ps_ref[:, g * gd:(g + 1) * gd]
        act_ref[:, g * gd:(g + 1) * gd] = (y * _silu(z)).astype(BF16)
        yield

    def finish_after(g):
        if g != len(POOL_WINDOWS) - 1:
            return None

        def finish():
            mix = _dot(act_ref[...], wo_ref[...])
            store(_rmsnorm(x1 + gate_ref[0] * mix, fw_ref[...]))

        return finish

    return _pipeline(list(range(len(POOL_WINDOWS))), issue, prepare, first, second, finish_after)


def _bwd_body(x_ref, of_ref, fb_ref, v_ref, q_ref, zg_ref, gate0_ref, lg_ref, cum_ref, keep_ref,
              s0_ref, gw_ref, wo_ref, shift1_ref, scale1_ref, gate1_ref, nw_ref, wi_ref,
              band_ref, inv_ref, wg_ref, ps_ref, wo1_ref, fw_ref, out_ref, st_ref, act0_ref,
              hb1_ref, act1_ref):
    d = x_ref.shape[-1]
    dh = d // HEADS
    gw = HEADS_PER_GROUP * dh
    _load_initial_state(s0_ref, st_ref)
    masks = (cum_ref.at[1], keep_ref.at[1])

    def fetch(s, g):
        at = (0, _subtile_rows(s, BWD_SUBTILE), slice(g * gw, (g + 1) * gw))
        yield
        return fb_ref[at].astype(F32), v_ref[at], q_ref[at].astype(F32)

    def emit(s, h, o):
        at = (0, _subtile_rows(s, BWD_SUBTILE), slice(h * dh, (h + 1) * dh))
        o = _rmsnorm(of_ref[at].astype(F32) + o, gw_ref[:, at[2]])
        act0_ref[at[1:]] = (o * zg_ref[at].astype(F32)).astype(BF16)

    def all_layers(s):
        yield from _hgrn_items(1, [s], fetch, lg_ref, masks, st_ref, emit, order, BWD_RHS_READY)
        rows = _subtile_rows(s, BWD_SUBTILE)
        x1 = x_ref[0, rows] + gate0_ref[0] * _dot(act0_ref[rows], wo_ref[...])
        yield

        def store(out):
            out_ref[0, rows] = out

        yield from _pool_stream(x1, shift1_ref, scale1_ref, gate1_ref, nw_ref.at[1:2], wi_ref,
                                band_ref, inv_ref, wg_ref, ps_ref, wo1_ref, fw_ref,
                                hb1_ref.at[rows], act1_ref.at[rows], store)

    subtiles = list(range(x_ref.shape[1] // BWD_SUBTILE))[::-1]
    order = _StateOrder(subtiles)
    _interleave(*[all_layers(s) for s in subtiles], lag=STREAM_LAG)


def _bwd_sweep(x, saved, mods, lb_logits, cum, keep, s0_b, gnorm_w, w_out, norm_w, pool_w_in,
               band, inv_count, w_grp, pool_scale, pool_w_out, final_w):
    bsz, t, d = x.shape
    dh = d // HEADS
    tq = BWD_TOKENS_PER_STEP
    nt = t // tq
    tile = pl.BlockSpec((1, tq, d), lambda b, i: (b, nt - 1 - i, 0))
    vec = pl.BlockSpec((1, d), lambda b, i: (0, 0))
    whole = _resident
    return pl.pallas_call(
        _bwd_body,
        grid=(bsz, nt),
        in_specs=[tile] * 6 + [
            _mod_spec(mods, 0, 2), whole(lb_logits), whole(cum), whole(keep),
            pl.BlockSpec((1, HEADS, dh, dh), lambda b, i: (b, 0, 0, 0)),
            vec, whole(w_out), _mod_spec(mods, 1, 0), _mod_spec(mods, 1, 1), _mod_spec(mods, 1, 2),
            whole(norm_w), whole(pool_w_in), whole(band),
            whole(inv_count), whole(w_grp), vec, whole(pool_w_out), vec,
        ],
        out_specs=tile,
        out_shape=jax.ShapeDtypeStruct((bsz, t, d), F32),
        scratch_shapes=[pltpu.VMEM((HEADS, dh, dh), F32)] + [pltpu.VMEM((tq, d), BF16)] * 3,
        compiler_params=_COMPILER_PARAMS,
        name="hgrn_bwd_pool_sweep",
    )(x, *saved, mods, lb_logits, cum, keep, s0_b, gnorm_w, w_out, mods, mods, mods,
      norm_w, pool_w_in, band, inv_count, w_grp, pool_scale, pool_w_out, final_w)


def _cumulative_matrices(n):
    r = jnp.arange(n)[:, None]
    c = jnp.arange(n)[None, :]
    same = (r // CHUNK) == (c // CHUNK)
    cum = jnp.stack([same & (c <= r), same & (c >= r)])
    return cum.astype(BF16), cum[:, :INTRA_ROWS, :INTRA_ROWS].astype(F32)


def _pool_windows(n):
    r = jnp.arange(n)[:, None]
    c = jnp.arange(n)[None, :]
    same = (r // GRID_W) == (c // GRID_W)
    band = jnp.stack([(same & (c >= r - w // 2) & (c < r - w // 2 + w)) for w in POOL_WINDOWS])
    count = jnp.sum(band.astype(F32), axis=-1, keepdims=True)
    return band.astype(BF16), 1.0 / count


def kernel(x, c, ctx, c_ctx, ada_w, ada_b, norm_w, hgrn_w_in, hgrn_lb_logits, hgrn_gnorm_w,
           hgrn_w_out, pool_w_in, pool_w_grp, pool_scale, pool_w_out, final_norm_w):
    bsz, t, d = x.shape
    depth = ada_w.shape[0]
    assert depth == 2 and hgrn_w_in.shape[0] == 1 and pool_w_in.shape[0] == 1
    assert t % FWD_TOKENS_PER_STEP == 0 and t % BWD_TOKENS_PER_STEP == 0
    assert ctx.shape[1] % MASK_ROWS == 0 and bsz < COND_ROWS
    assert bsz % CTX_BATCH_PER_STEP == 0 and ada_w.shape[2] == 3 * d
    assert HEADS_PER_GROUP * (d // HEADS) == MXU_COLUMNS_V7X
    assert FWD_TOKENS_PER_STEP % FWD_SUBTILE == 0 and BWD_TOKENS_PER_STEP % BWD_SUBTILE == 0
    assert FWD_SUBTILE % MASK_ROWS == 0 and BWD_SUBTILE % MASK_ROWS == 0
    assert MASK_ROWS % INTRA_ROWS == 0 and INTRA_ROWS % CHUNK == 0 and MASK_ROWS % GRID_W == 0

    cond = jnp.zeros((COND_ROWS, d), F32).at[:bsz].set(c).at[bsz].set(c_ctx)
    mods = _ada_mods(cond, ada_w, ada_b)

    w_in = hgrn_w_in[0].astype(BF16)
    lb_logits = hgrn_lb_logits
    cum, keep = _cumulative_matrices(MASK_ROWS)
    band, inv_count = _pool_windows(MASK_ROWS)

    s0_f, s0_b = _ctx_states(ctx, mods, bsz, norm_w, w_in, lb_logits, cum, keep)
    saved = _fwd_sweep(x, mods, norm_w, w_in, lb_logits, cum, keep, s0_f)
    return _bwd_sweep(x, saved, mods, lb_logits, cum, keep, s0_b, hgrn_gnorm_w[0:1],
                      hgrn_w_out[0].astype(BF16), norm_w, pool_w_in[0].astype(BF16), band,
                      inv_count, pool_w_grp[0].astype(BF16), pool_scale[0:1],
                      pool_w_out[0].astype(BF16), final_norm_w.reshape(1, d))
```

```python
import jax
import jax.numpy as jnp
from jax import lax
from jax.experimental import pallas as pl
from jax.experimental.pallas import tpu as pltpu

HEADS = 8
CHUNK = 64
GRID_W = 64
POOL_WINDOWS = (2, 4, 8, 16)
EPS = 1e-6
MXU_COLUMNS_V7X = 256
VMEM_BYTES_V7X = 64 * 1024 * 1024
MASK_ROWS = 256
FWD_SUBTILE = 512
BWD_SUBTILE = 256
FWD_TOKENS_PER_STEP = 1024
BWD_TOKENS_PER_STEP = 1024
HEADS_PER_GROUP = 2
INTRA_ROWS = 128
LOOKAHEAD = 2
FWD_LOOKAHEAD = 3
FWD_RHS_READY = (True, True)
BWD_RHS_READY = (False, True)
STREAM_LAG = 20
CTX_BATCH_PER_STEP = 4
COND_ROWS = 16
VMEM_LIMIT_BYTES = VMEM_BYTES_V7X - 4 * 1024 * 1024

F32 = jnp.float32
BF16 = jnp.bfloat16

_COMPILER_PARAMS = pltpu.CompilerParams(vmem_limit_bytes=VMEM_LIMIT_BYTES)


def _silu(a):
    half = 0.5 * a
    return half + half * jnp.tanh(half)


def _rmsnorm(xf, w):
    y = xf * lax.rsqrt(jnp.mean(xf * xf, axis=-1, keepdims=True) + EPS)
    return y * w


def _normalised_rows(x, shift_ref, scale_ref, nw_ref):
    gain = nw_ref[...] * (1.0 + scale_ref[0])
    return (_rmsnorm(x, gain) + shift_ref[0]).astype(BF16)


def _dot(a, b):
    return jnp.dot(a, b, preferred_element_type=F32)


def _dot_nt(a, b):
    return lax.dot_general(a, b, (((1,), (1,)), ((), ())), preferred_element_type=F32)


def _dot_tn(a, b):
    return lax.dot_general(a, b, (((0,), (0,)), ((), ())), preferred_element_type=F32)


def _lower_bound(lg_ref, direction, layer):
    rows = [lg_ref[n, direction:direction + 1, :] for n in range(lg_ref.shape[0])]
    m = jnp.zeros_like(rows[0])
    for r in rows:
        m = jnp.maximum(m, r)
    exps = [jnp.exp(r - m) for r in rows]
    denom = jnp.exp(-m)
    for e in exps:
        denom = denom + e
    acc = exps[0]
    for e in exps[1:layer + 1]:
        acc = acc + e
    return acc / denom


def _subtile_rows(s, size):
    return slice(s * size, (s + 1) * size)


def _by_mask_rows(fn, *arrays):
    n = arrays[0].shape[0] // MASK_ROWS
    blocks = [fn(*(a[i * MASK_ROWS:(i + 1) * MASK_ROWS] for a in arrays)) for i in range(n)]
    return blocks[0] if n == 1 else jnp.concatenate(blocks, axis=0)


def _resident(a):
    return pl.BlockSpec(a.shape, lambda b, i: (0,) * a.ndim, pipeline_mode=pl.Buffered(1))


class _Steps:
    def __init__(self, gen):
        self._gen = gen
        self.done = False
        self.value = None

    def step(self):
        if not self.done:
            try:
                next(self._gen)
            except StopIteration as stop:
                self.done = True
                self.value = stop.value
        return not self.done


def _alongside(main, side):
    while main.step():
        yield
        if side.step():
            yield
    return main.value


def _pipeline(items, issue, prepare, first, second, finish_after, lookahead=LOOKAHEAD):
    ahead = min(lookahead, len(items))
    issued = []
    for item in items[:ahead]:
        issued.append((yield from issue(item)))
    prepared = []
    for item, handle in zip(items, issued):
        prepared.append(prepare(item, handle))
        yield
    deferred = []
    for i, item in enumerate(items):
        later = items[i + ahead] if i + ahead < len(items) else None
        issuing = _Steps(issue(later) if later is not None else iter(()))
        cont = yield from _alongside(_Steps(first(item, prepared[i])), issuing)
        for fn in deferred:
            fn()
            yield
        deferred = []
        yield from _alongside(_Steps(second(cont)), issuing)
        while issuing.step():
            yield
        fn = finish_after(item)
        if fn is not None:
            deferred.append(fn)
        if later is not None:
            prepared.append(prepare(later, issuing.value))
            yield
    for fn in deferred:
        fn()
        yield


def _interleave(*streams, lag=0):
    done = [False] * len(streams)
    rnd = 0
    while not all(done):
        for k, stream in enumerate(streams):
            if done[k] or rnd < k * lag:
                continue
            try:
                next(stream)
            except StopIteration:
                done[k] = True
        rnd += 1


def _head_operands(reverse, b, k, q, rhs_ready):
    nc = b.shape[0] // CHUNK
    mid = CHUNK // 2 if reverse else CHUNK // 2 - 1
    last = 0 if reverse else CHUNK - 1
    if q is None:
        b_last = [b[n * CHUNK + last:n * CHUNK + last + 1] for n in range(nc)]
        to_end, total = {}, jnp.zeros_like(b_last[0])
        for n in (range(nc) if reverse else range(nc - 1, -1, -1)):
            to_end[n] = total
            total = total + b_last[n]
        k_end = [k[n * CHUNK:(n + 1) * CHUNK]
                 * jnp.exp2(b_last[n] + to_end[n] - b[n * CHUNK:(n + 1) * CHUNK])
                 for n in range(nc)]
        return [], [], [jnp.concatenate(k_end, axis=0).astype(BF16)], [], [jnp.exp2(total)]
    k_dec, q_dec, k_end, q_in, decay = [], [], [], [], []
    for n in range(nc):
        rows = slice(n * CHUNK, (n + 1) * CHUNK)
        bn = b[rows]
        b_mid = bn[mid:mid + 1]
        b_last = bn[last:last + 1]
        kd = k[rows] * jnp.exp2(b_mid - bn)
        k_dec.append(kd if rhs_ready[0] else kd.astype(BF16))
        k_end.append((kd * jnp.exp2(b_last - b_mid)).astype(BF16))
        decay.append(jnp.exp2(b_last))
        qd = q[rows] * jnp.exp2(bn - b_mid)
        q_dec.append(qd.astype(BF16))
        q_in.append((qd * jnp.exp2(b_mid)).astype(BF16))
    if rhs_ready[0]:
        per_block = INTRA_ROWS // CHUNK
        k_dec = [jnp.concatenate(k_dec[j:j + per_block], axis=0).T.astype(BF16)
                 for j in range(0, nc, per_block)]
    return k_dec, q_dec, k_end, q_in, decay


def _scan_head(reverse, operands, v, keep, st, rhs_ready):
    k_dec, q_dec, k_end, q_in, decay = operands
    nc = len(decay)
    per_block = INTRA_ROWS // CHUNK
    scores = []
    if q_dec:
        for j in range(nc // per_block):
            blk = slice(j * per_block, (j + 1) * per_block)
            q_blk = jnp.concatenate(q_dec[blk], axis=0)
            if rhs_ready[0]:
                scores.append(_dot(q_blk, k_dec[j]))
            else:
                scores.append(_dot_nt(q_blk, jnp.concatenate(k_dec[blk], axis=0)))
        yield
    span = v.shape[0] // nc
    update = [_dot_tn(v[n * span:(n + 1) * span], k_end[n]) for n in range(nc)]
    yield
    if q_dec:
        scores = [jnp.where(keep, a, 0.0).astype(BF16) for a in scores]
        o_intra = [_dot(a, v[j * INTRA_ROWS:(j + 1) * INTRA_ROWS]) for j, a in enumerate(scores)]
        yield
    carried = [None] * nc
    for n in (range(nc - 1, -1, -1) if reverse else range(nc)):
        carried[n] = st.T.astype(BF16) if rhs_ready[1] else st.astype(BF16)
        st = st * decay[n] + update[n]
    if not q_dec:
        return None, st
    o_inter = [(_dot if rhs_ready[1] else _dot_nt)(q_in[n], carried[n]) for n in range(nc)]
    return jnp.concatenate(o_intra, axis=0) + jnp.concatenate(o_inter, axis=0), st


class _StateOrder:
    def __init__(self, subtiles):
        self._before = {s: subtiles[:i] for i, s in enumerate(subtiles)}
        self._written = set()

    def read(self, s, h):
        missing = [p for p in self._before[s] if (p, h) not in self._written]
        assert not missing, f"state of head {h} read for subtile {s} before {missing} wrote it"

    def wrote(self, s, h):
        self._written.add((s, h))


def _hgrn_items(direction, subtiles, fetch, lg_ref, masks, st_ref, emit, order=None,
                rhs_ready=FWD_RHS_READY, lookahead=LOOKAHEAD):
    dh = st_ref.shape[-1]
    gw = HEADS_PER_GROUP * dh
    reverse = direction == 1
    cum_ref, keep_ref = masks
    cum = cum_ref[...]
    keep = keep_ref[...] != 0.0
    lb_all = _lower_bound(lg_ref, direction, 0)

    def issue(item):
        return (yield from fetch(*item))

    def prepare(item, fetched):
        s, g = item
        f_pre, v, q = fetched
        lb = lb_all[:, g * gw:(g + 1) * gw]
        f = 0.5 * (1.0 + lb) + (0.5 * (1.0 - lb)) * jnp.tanh(0.5 * f_pre)
        b = _by_mask_rows(lambda logf: _dot(cum, logf), jnp.log2(f).astype(BF16))
        k = 1.0 - f
        operands = []
        for hh in range(HEADS_PER_GROUP):
            lanes = slice(hh * dh, (hh + 1) * dh)
            operands.append(_head_operands(reverse, b[:, lanes], k[:, lanes],
                                           None if q is None else q[:, lanes], rhs_ready))
        return operands, v, q is not None

    def first(item, prepared):
        s, g = item
        operands, v, has_q = prepared
        heads = []
        for hh in range(HEADS_PER_GROUP):
            lanes = slice(hh * dh, (hh + 1) * dh)
            if order is not None:
                order.read(s, g * HEADS_PER_GROUP + hh)
            head = _Steps(_scan_head(reverse, operands[hh], v[:, lanes], keep,
                                     st_ref[g * HEADS_PER_GROUP + hh], rhs_ready))
            for _ in range(2 if has_q else 1):
                head.step()
                yield
            heads.append(head)
        return item, heads

    def second(cont):
        (s, g), heads = cont
        for hh, head in enumerate(heads):
            h = g * HEADS_PER_GROUP + hh
            while head.step():
                yield
            o, st = head.value
            st_ref[h] = st
            if order is not None:
                order.wrote(s, h)
            if emit is not None:
                emit(s, h, o)
            yield

    items = [(s, g) for s in subtiles for g in range(HEADS // HEADS_PER_GROUP)]
    return _pipeline(items, issue, prepare, first, second, lambda item: None, lookahead)


def _ada_body(cond_ref, w_ref, b_ref, o_ref):
    o_ref[:, 0, :] = _dot(_silu(cond_ref[...]), w_ref[0]) + b_ref[0]


def _ada_mods(cond, ada_w, ada_b):
    depth, d, n = ada_w.shape
    return pl.pallas_call(
        _ada_body,
        grid=(depth, n // d),
        in_specs=[
            pl.BlockSpec((COND_ROWS, d), lambda l, j: (0, 0)),
            pl.BlockSpec((1, d, d), lambda l, j: (l, 0, j)),
            pl.BlockSpec((1, 1, d), lambda l, j: (l, 0, j)),
        ],
        out_specs=pl.BlockSpec((COND_ROWS, 1, d), lambda l, j: (l * (n // d) + j, 0, 0)),
        out_shape=jax.ShapeDtypeStruct((depth * (n // d) * COND_ROWS, 1, d), F32),
        compiler_params=_COMPILER_PARAMS,
        name="ada_mods",
    )(cond, ada_w, ada_b.reshape(depth, 1, n))


def _mod_spec(mods, layer, which, row=None):
    base = (layer * 3 + which) * COND_ROWS
    if row is None:
        return pl.BlockSpec((1, 1, mods.shape[-1]), lambda b, *_: (base + b, 0, 0))
    return pl.BlockSpec((1, 1, mods.shape[-1]), lambda *_: (base + row, 0, 0))


def _ctx_body(ctx_ref, shift_ref, scale_ref, nw_ref, w_ref, lg_ref, cum_ref, keep_ref,
              sf_ref, sb_ref, hb_ref):
    n_batch, tc, d = ctx_ref.shape
    gw = HEADS_PER_GROUP * (d // HEADS)
    subtiles = list(range(tc // MASK_ROWS))
    normalised, v = set(), {}

    def fetch(direction, bi, s, g):
        rows = _subtile_rows(s, MASK_ROWS)
        hb = hb_ref.at[bi]
        if (bi, s) not in normalised:
            hb[rows] = _normalised_rows(ctx_ref[bi, rows], shift_ref, scale_ref,
                                        nw_ref.at[0:1])
            normalised.add((bi, s))
        cols = lambda block: slice(block * d + g * gw, block * d + (g + 1) * gw)
        if (bi, s, g) not in v:
            v[bi, s, g] = _dot(hb[rows], w_ref[:, cols(2)]).astype(BF16)
            yield
        f_pre = _dot(hb[rows], w_ref[:, cols(direction)])
        yield
        return f_pre, v[bi, s, g], None

    streams = []
    for bi in range(n_batch):
        for direction, st_ref in enumerate((sf_ref, sb_ref)):
            st = st_ref.at[bi]
            st[...] = jnp.zeros(st.shape, F32)
            order = subtiles[::-1] if direction else subtiles
            streams.append(_hgrn_items(
                direction, order, lambda s, g, dr=direction, bi=bi: fetch(dr, bi, s, g), lg_ref,
                (cum_ref.at[direction], keep_ref.at[direction]), st, None))
    _interleave(*streams)


def _ctx_states(ctx, mods, ctx_row, norm_w, w_in, lb_logits, cum, keep):
    bsz, tc, d = ctx.shape
    dh = d // HEADS
    const2 = lambda b: (0, 0)
    nb = CTX_BATCH_PER_STEP
    state_spec = pl.BlockSpec((nb, HEADS, dh, dh), lambda b: (b, 0, 0, 0))
    state_shape = jax.ShapeDtypeStruct((bsz, HEADS, dh, dh), F32)
    return pl.pallas_call(
        _ctx_body,
        grid=(bsz // nb,),
        in_specs=[
            pl.BlockSpec((nb, tc, d), lambda b: (b, 0, 0)),
            _mod_spec(mods, 0, 0, ctx_row), _mod_spec(mods, 0, 1, ctx_row),
            pl.BlockSpec(norm_w.shape, const2),
            pl.BlockSpec((d, 3 * d), const2),
            pl.BlockSpec(lb_logits.shape, lambda b: (0, 0, 0)),
            pl.BlockSpec(cum.shape, lambda b: (0, 0, 0)),
            pl.BlockSpec(keep.shape, lambda b: (0, 0, 0)),
        ],
        out_specs=[state_spec, state_spec],
        out_shape=[state_shape, state_shape],
        scratch_shapes=[pltpu.VMEM((nb, tc, d), BF16)],
        compiler_params=_COMPILER_PARAMS,
        name="ctx_states",
    )(ctx, mods, mods, norm_w, w_in, lb_logits, cum, keep)


def _load_initial_state(s0_ref, st_ref):
    @pl.when(pl.program_id(1) == 0)
    def _():
        st_ref[...] = s0_ref[0]


def _fwd_body(x_ref, shift_ref, scale_ref, nw_ref, w_ref, lg_ref, cum_ref, keep_ref, s0_ref,
              of_ref, fb_ref, v_ref, q_ref, zg_ref, st_ref, hb_ref):
    d = x_ref.shape[-1]
    dh = d // HEADS
    gw = HEADS_PER_GROUP * dh
    _load_initial_state(s0_ref, st_ref)
    normalised = set()

    def fetch(s, g):
        rows = _subtile_rows(s, FWD_SUBTILE)
        if s not in normalised:
            hb_ref[rows] = _normalised_rows(x_ref[0, rows], shift_ref, scale_ref,
                                            nw_ref.at[0:1])
            normalised.add(s)
        lanes = slice(g * gw, (g + 1) * gw)
        project = lambda block: _dot(
            hb_ref[rows], w_ref[:, block * d + g * gw:block * d + (g + 1) * gw])
        f_fwd = project(0)
        yield
        v = project(2).astype(BF16)
        v_ref[0, rows, lanes] = v
        yield
        q = _silu(project(3)) * (dh ** -0.5)
        q_ref[0, rows, lanes] = q.astype(BF16)
        yield
        fb_ref[0, rows, lanes] = project(1).astype(BF16)
        yield
        zg_ref[0, rows, lanes] = _silu(project(4)).astype(BF16)
        yield
        return f_fwd, v, q

    def emit(s, h, o):
        of_ref[0, _subtile_rows(s, FWD_SUBTILE), h * dh:(h + 1) * dh] = o.astype(BF16)

    subtiles = list(range(x_ref.shape[1] // FWD_SUBTILE))
    _interleave(_hgrn_items(0, subtiles, fetch, lg_ref, (cum_ref.at[0], keep_ref.at[0]), st_ref,
                            emit, lookahead=FWD_LOOKAHEAD))


def _fwd_sweep(x, mods, norm_w, w_in, lb_logits, cum, keep, s0_f):
    bsz, t, d = x.shape
    dh = d // HEADS
    tq = FWD_TOKENS_PER_STEP
    tile = pl.BlockSpec((1, tq, d), lambda b, i: (b, i, 0))
    saved = jax.ShapeDtypeStruct((bsz, t, d), BF16)
    return pl.pallas_call(
        _fwd_body,
        grid=(bsz, t // tq),
        in_specs=[
            tile, _mod_spec(mods, 0, 0), _mod_spec(mods, 0, 1), _resident(norm_w),
            _resident(w_in), _resident(lb_logits), _resident(cum), _resident(keep),
            pl.BlockSpec((1, HEADS, dh, dh), lambda b, i: (b, 0, 0, 0)),
        ],
        out_specs=[tile] * 5,
        out_shape=[saved] * 5,
        scratch_shapes=[pltpu.VMEM((HEADS, dh, dh), F32), pltpu.VMEM((tq, d), BF16)],
        compiler_params=_COMPILER_PARAMS,
        name="hgrn_fwd_sweep",
    )(x, mods, mods, norm_w, w_in, lb_logits, cum, keep, s0_f)


def _pool_stream(x1, shift_ref, scale_ref, gate_ref, nw_ref, wi_ref, band_ref, inv_ref, wg_ref,
                 ps_ref, wo_ref, fw_ref, hb_ref, act_ref, store):
    d = x1.shape[-1]
    gd = d // len(POOL_WINDOWS)
    hb_ref[...] = _normalised_rows(x1, shift_ref, scale_ref, nw_ref)

    def issue(g):
        u = _dot(hb_ref[...], wi_ref[:, g * gd:(g + 1) * gd])
        yield
        z = _dot(hb_ref[...], wi_ref[:, d + g * gd:d + (g + 1) * gd])
        yield
        return u, z

    def prepare(g, projected):
        u, z = projected
        return u, z, u.astype(BF16)

    def first(g, prepared):
        u, z, u_bf = prepared
        band = band_ref[g]
        mean = _by_mask_rows(lambda rows: _dot(band, rows) * inv_ref[g], u_bf)
        yield
        return g, u, z, mean

    def second(cont):
        g, u, z, mean = cont
        pooled = mean - u
        y = _dot(pooled.astype(BF16), wg_ref[g]) * ps_ref[:, g * gd:(g + 1) * gd]
        act_ref[:, g * gd:(g + 1) * gd] = (y * _silu(z)).astype(BF16)
        yield

    def finish_after(g):
        if g != len(POOL_WINDOWS) - 1:
            return None

        def finish():
            mix = _dot(act_ref[...], wo_ref[...])
            store(_rmsnorm(x1 + gate_ref[0] * mix, fw_ref[...]))

        return finish

    return _pipeline(list(range(len(POOL_WINDOWS))), issue, prepare, first, second, finish_after)


def _bwd_body(x_ref, of_ref, fb_ref, v_ref, q_ref, zg_ref, gate0_ref, lg_ref, cum_ref, keep_ref,
              s0_ref, gw_ref, wo_ref, shift1_ref, scale1_ref, gate1_ref, nw_ref, wi_ref,
              band_ref, inv_ref, wg_ref, ps_ref, wo1_ref, fw_ref, out_ref, st_ref, act0_ref,
              hb1_ref, act1_ref):
    d = x_ref.shape[-1]
    dh = d // HEADS
    gw = HEADS_PER_GROUP * dh
    _load_initial_state(s0_ref, st_ref)
    masks = (cum_ref.at[1], keep_ref.at[1])

    def fetch(s, g):
        at = (0, _subtile_rows(s, BWD_SUBTILE), slice(g * gw, (g + 1) * gw))
        yield
        return fb_ref[at].astype(F32), v_ref[at], q_ref[at].astype(F32)

    def emit(s, h, o):
        at = (0, _subtile_rows(s, BWD_SUBTILE), slice(h * dh, (h + 1) * dh))
        o = _rmsnorm(of_ref[at].astype(F32) + o, gw_ref[:, at[2]])
        act0_ref[at[1:]] = (o * zg_ref[at].astype(F32)).astype(BF16)

    def all_layers(s):
        yield from _hgrn_items(1, [s], fetch, lg_ref, masks, st_ref, emit, order, BWD_RHS_READY)
        rows = _subtile_rows(s, BWD_SUBTILE)
        x1 = x_ref[0, rows] + gate0_ref[0] * _dot(act0_ref[rows], wo_ref[...])
        yield

        def store(out):
            out_ref[0, rows] = out

        yield from _pool_stream(x1, shift1_ref, scale1_ref, gate1_ref, nw_ref.at[1:2], wi_ref,
                                band_ref, inv_ref, wg_ref, ps_ref, wo1_ref, fw_ref,
                                hb1_ref.at[rows], act1_ref.at[rows], store)

    subtiles = list(range(x_ref.shape[1] // BWD_SUBTILE))[::-1]
    order = _StateOrder(subtiles)
    _interleave(*[all_layers(s) for s in subtiles], lag=STREAM_LAG)


def _bwd_sweep(x, saved, mods, lb_logits, cum, keep, s0_b, gnorm_w, w_out, norm_w, pool_w_in,
               band, inv_count, w_grp, pool_scale, pool_w_out, final_w):
    bsz, t, d = x.shape
    dh = d // HEADS
    tq = BWD_TOKENS_PER_STEP
    nt = t // tq
    tile = pl.BlockSpec((1, tq, d), lambda b, i: (b, nt - 1 - i, 0))
    vec = pl.BlockSpec((1, d), lambda b, i: (0, 0))
    whole = _resident
    return pl.pallas_call(
        _bwd_body,
        grid=(bsz, nt),
        in_specs=[tile] * 6 + [
            _mod_spec(mods, 0, 2), whole(lb_logits), whole(cum), whole(keep),
            pl.BlockSpec((1, HEADS, dh, dh), lambda b, i: (b, 0, 0, 0)),
            vec, whole(w_out), _mod_spec(mods, 1, 0), _mod_spec(mods, 1, 1), _mod_spec(mods, 1, 2),
            whole(norm_w), whole(pool_w_in), whole(band),
            whole(inv_count), whole(w_grp), vec, whole(pool_w_out), vec,
        ],
        out_specs=tile,
        out_shape=jax.ShapeDtypeStruct((bsz, t, d), F32),
        scratch_shapes=[pltpu.VMEM((HEADS, dh, dh), F32)] + [pltpu.VMEM((tq, d), BF16)] * 3,
        compiler_params=_COMPILER_PARAMS,
        name="hgrn_bwd_pool_sweep",
    )(x, *saved, mods, lb_logits, cum, keep, s0_b, gnorm_w, w_out, mods, mods, mods,
      norm_w, pool_w_in, band, inv_count, w_grp, pool_scale, pool_w_out, final_w)


def _cumulative_matrices(n):
    r = jnp.arange(n)[:, None]
    c = jnp.arange(n)[None, :]
    same = (r // CHUNK) == (c // CHUNK)
    cum = jnp.stack([same & (c <= r), same & (c >= r)])
    return cum.astype(BF16), cum[:, :INTRA_ROWS, :INTRA_ROWS].astype(F32)


def _pool_windows(n):
    r = jnp.arange(n)[:, None]
    c = jnp.arange(n)[None, :]
    same = (r // GRID_W) == (c // GRID_W)
    band = jnp.stack([(same & (c >= r - w // 2) & (c < r - w // 2 + w)) for w in POOL_WINDOWS])
    count = jnp.sum(band.astype(F32), axis=-1, keepdims=True)
    return band.astype(BF16), 1.0 / count


def kernel(x, c, ctx, c_ctx, ada_w, ada_b, norm_w, hgrn_w_in, hgrn_lb_logits, hgrn_gnorm_w,
           hgrn_w_out, pool_w_in, pool_w_grp, pool_scale, pool_w_out, final_norm_w):
    bsz, t, d = x.shape
    depth = ada_w.shape[0]
    assert depth == 2 and hgrn_w_in.shape[0] == 1 and pool_w_in.shape[0] == 1
    assert t % FWD_TOKENS_PER_STEP == 0 and t % BWD_TOKENS_PER_STEP == 0
    assert ctx.shape[1] % MASK_ROWS == 0 and bsz < COND_ROWS
    assert bsz % CTX_BATCH_PER_STEP == 0 and ada_w.shape[2] == 3 * d
    assert HEADS_PER_GROUP * (d // HEADS) == MXU_COLUMNS_V7X
    assert FWD_TOKENS_PER_STEP % FWD_SUBTILE == 0 and BWD_TOKENS_PER_STEP % BWD_SUBTILE == 0
    assert FWD_SUBTILE % MASK_ROWS == 0 and BWD_SUBTILE % MASK_ROWS == 0
    assert MASK_ROWS % INTRA_ROWS == 0 and INTRA_ROWS % CHUNK == 0 and MASK_ROWS % GRID_W == 0

    cond = jnp.zeros((COND_ROWS, d), F32).at[:bsz].set(c).at[bsz].set(c_ctx)
    mods = _ada_mods(cond, ada_w, ada_b)

    w_in = hgrn_w_in[0].astype(BF16)
    lb_logits = hgrn_lb_logits
    cum, keep = _cumulative_matrices(MASK_ROWS)
    band, inv_count = _pool_windows(MASK_ROWS)

    s0_f, s0_b = _ctx_states(ctx, mods, bsz, norm_w, w_in, lb_logits, cum, keep)
    saved = _fwd_sweep(x, mods, norm_w, w_in, lb_logits, cum, keep, s0_f)
    return _bwd_sweep(x, saved, mods, lb_logits, cum, keep, s0_b, hgrn_gnorm_w[0:1],
                      hgrn_w_out[0].astype(BF16), norm_w, pool_w_in[0].astype(BF16), band,
                      inv_count, pool_w_grp[0].astype(BF16), pool_scale[0:1],
                      pool_w_out[0].astype(BF16), final_norm_w.reshape(1, d))
```

```python
import jax
import jax.numpy as jnp
from jax import lax
from jax.experimental import pallas as pl
from jax.experimental.pallas import tpu as pltpu

HEADS = 8
CHUNK = 64
GRID_W = 64
POOL_WINDOWS = (2, 4, 8, 16)
EPS = 1e-6
MXU_COLUMNS_V7X = 256
VMEM_BYTES_V7X = 64 * 1024 * 1024
MASK_ROWS = 256
FWD_SUBTILE = 512
BWD_SUBTILE = 256
FWD_TOKENS_PER_STEP = 1024
BWD_TOKENS_PER_STEP = 1024
HEADS_PER_GROUP = 2
INTRA_ROWS = 128
LOOKAHEAD = 2
RECURRENCE_LOOKAHEAD = 3
FWD_RHS_READY = (True, True)
BWD_RHS_READY = (False, True)
STREAM_LAG = 20
CTX_BATCH_PER_STEP = 4
COND_ROWS = 16
VMEM_LIMIT_BYTES = VMEM_BYTES_V7X - 4 * 1024 * 1024

F32 = jnp.float32
BF16 = jnp.bfloat16

_COMPILER_PARAMS = pltpu.CompilerParams(vmem_limit_bytes=VMEM_LIMIT_BYTES)


def _silu(a):
    half = 0.5 * a
    return half + half * jnp.tanh(half)


def _rmsnorm(xf, w):
    y = xf * lax.rsqrt(jnp.mean(xf * xf, axis=-1, keepdims=True) + EPS)
    return y * w


def _normalised_rows(x, shift_ref, scale_ref, nw_ref):
    gain = nw_ref[...] * (1.0 + scale_ref[0])
    return (_rmsnorm(x, gain) + shift_ref[0]).astype(BF16)


def _dot(a, b):
    return jnp.dot(a, b, preferred_element_type=F32)


def _dot_nt(a, b):
    return lax.dot_general(a, b, (((1,), (1,)), ((), ())), preferred_element_type=F32)


def _dot_tn(a, b):
    return lax.dot_general(a, b, (((0,), (0,)), ((), ())), preferred_element_type=F32)


def _lower_bound(lg_ref, direction, layer):
    rows = [lg_ref[n, direction:direction + 1, :] for n in range(lg_ref.shape[0])]
    m = jnp.zeros_like(rows[0])
    for r in rows:
        m = jnp.maximum(m, r)
    exps = [jnp.exp(r - m) for r in rows]
    denom = jnp.exp(-m)
    for e in exps:
        denom = denom + e
    acc = exps[0]
    for e in exps[1:layer + 1]:
        acc = acc + e
    return acc / denom


def _subtile_rows(s, size):
    return slice(s * size, (s + 1) * size)


def _by_mask_rows(fn, *arrays):
    n = arrays[0].shape[0] // MASK_ROWS
    blocks = [fn(*(a[i * MASK_ROWS:(i + 1) * MASK_ROWS] for a in arrays)) for i in range(n)]
    return blocks[0] if n == 1 else jnp.concatenate(blocks, axis=0)


def _resident(a):
    return pl.BlockSpec(a.shape, lambda b, i: (0,) * a.ndim, pipeline_mode=pl.Buffered(1))


class _Steps:
    def __init__(self, gen):
        self._gen = gen
        self.done = False
        self.value = None

    def step(self):
        if not self.done:
            try:
                next(self._gen)
            except StopIteration as stop:
                self.done = True
                self.value = stop.value
        return not self.done


def _alongside(main, side):
    while main.step():
        yield
        if side.step():
            yield
    return main.value


def _pipeline(items, issue, prepare, first, second, finish_after, lookahead=LOOKAHEAD):
    ahead = min(lookahead, len(items))
    issued = []
    for item in items[:ahead]:
        issued.append((yield from issue(item)))
    prepared = []
    for item, handle in zip(items, issued):
        prepared.append(prepare(item, handle))
        yield
    deferred = []
    for i, item in enumerate(items):
        later = items[i + ahead] if i + ahead < len(items) else None
        issuing = _Steps(issue(later) if later is not None else iter(()))
        cont = yield from _alongside(_Steps(first(item, prepared[i])), issuing)
        for fn in deferred:
            fn()
            yield
        deferred = []
        yield from _alongside(_Steps(second(cont)), issuing)
        while issuing.step():
            yield
        fn = finish_after(item)
        if fn is not None:
            deferred.append(fn)
        if later is not None:
            prepared.append(prepare(later, issuing.value))
            yield
    for fn in deferred:
        fn()
        yield


def _interleave(*streams, lag=0):
    done = [False] * len(streams)
    rnd = 0
    while not all(done):
        for k, stream in enumerate(streams):
            if done[k] or rnd < k * lag:
                continue
            try:
                next(stream)
            except StopIteration:
                done[k] = True
        rnd += 1


def _head_operands(reverse, b, k, q, rhs_ready):
    nc = b.shape[0] // CHUNK
    mid = CHUNK // 2 if reverse else CHUNK // 2 - 1
    last = 0 if reverse else CHUNK - 1
    if q is None:
        b_last = [b[n * CHUNK + last:n * CHUNK + last + 1] for n in range(nc)]
        to_end, total = {}, jnp.zeros_like(b_last[0])
        for n in (range(nc) if reverse else range(nc - 1, -1, -1)):
            to_end[n] = total
            total = total + b_last[n]
        k_end = [k[n * CHUNK:(n + 1) * CHUNK]
                 * jnp.exp2(b_last[n] + to_end[n] - b[n * CHUNK:(n + 1) * CHUNK])
                 for n in range(nc)]
        return [], [], [jnp.concatenate(k_end, axis=0).astype(BF16)], [], [jnp.exp2(total)]
    k_dec, q_dec, k_end, q_in, decay = [], [], [], [], []
    for n in range(nc):
        rows = slice(n * CHUNK, (n + 1) * CHUNK)
        bn = b[rows]
        b_mid = bn[mid:mid + 1]
        b_last = bn[last:last + 1]
        kd = k[rows] * jnp.exp2(b_mid - bn)
        k_dec.append(kd if rhs_ready[0] else kd.astype(BF16))
        k_end.append((kd * jnp.exp2(b_last - b_mid)).astype(BF16))
        decay.append(jnp.exp2(b_last))
        qd = q[rows] * jnp.exp2(bn - b_mid)
        q_dec.append(qd.astype(BF16))
        q_in.append((qd * jnp.exp2(b_mid)).astype(BF16))
    if rhs_ready[0]:
        per_block = INTRA_ROWS // CHUNK
        k_dec = [jnp.concatenate(k_dec[j:j + per_block], axis=0).T.astype(BF16)
                 for j in range(0, nc, per_block)]
    return k_dec, q_dec, k_end, q_in, decay


def _scan_head(reverse, operands, v, keep, st, rhs_ready):
    k_dec, q_dec, k_end, q_in, decay = operands
    nc = len(decay)
    per_block = INTRA_ROWS // CHUNK
    scores = []
    if q_dec:
        for j in range(nc // per_block):
            blk = slice(j * per_block, (j + 1) * per_block)
            q_blk = jnp.concatenate(q_dec[blk], axis=0)
            if rhs_ready[0]:
                scores.append(_dot(q_blk, k_dec[j]))
            else:
                scores.append(_dot_nt(q_blk, jnp.concatenate(k_dec[blk], axis=0)))
        yield
    span = v.shape[0] // nc
    update = [_dot_tn(v[n * span:(n + 1) * span], k_end[n]) for n in range(nc)]
    yield
    if q_dec:
        scores = [jnp.where(keep, a, 0.0).astype(BF16) for a in scores]
        o_intra = [_dot(a, v[j * INTRA_ROWS:(j + 1) * INTRA_ROWS]) for j, a in enumerate(scores)]
        yield
    carried = [None] * nc
    for n in (range(nc - 1, -1, -1) if reverse else range(nc)):
        carried[n] = st.T.astype(BF16) if rhs_ready[1] else st.astype(BF16)
        st = st * decay[n] + update[n]
    if not q_dec:
        return None, st
    o_inter = [(_dot if rhs_ready[1] else _dot_nt)(q_in[n], carried[n]) for n in range(nc)]
    return jnp.concatenate(o_intra, axis=0) + jnp.concatenate(o_inter, axis=0), st


class _StateOrder:
    def __init__(self, subtiles):
        self._before = {s: subtiles[:i] for i, s in enumerate(subtiles)}
        self._written = set()

    def read(self, s, h):
        missing = [p for p in self._before[s] if (p, h) not in self._written]
        assert not missing, f"state of head {h} read for subtile {s} before {missing} wrote it"

    def wrote(self, s, h):
        self._written.add((s, h))


def _hgrn_items(direction, subtiles, fetch, lg_ref, masks, st_ref, emit, order=None,
                rhs_ready=FWD_RHS_READY, lookahead=LOOKAHEAD):
    dh = st_ref.shape[-1]
    gw = HEADS_PER_GROUP * dh
    reverse = direction == 1
    cum_ref, keep_ref = masks
    cum = cum_ref[...]
    keep = keep_ref[...] != 0.0
    lb_all = _lower_bound(lg_ref, direction, 0)

    def issue(item):
        return (yield from fetch(*item))

    def prepare(item, fetched):
        s, g = item
        f_pre, v, q = fetched
        lb = lb_all[:, g * gw:(g + 1) * gw]
        f = 0.5 * (1.0 + lb) + (0.5 * (1.0 - lb)) * jnp.tanh(0.5 * f_pre)
        b = _by_mask_rows(lambda logf: _dot(cum, logf), jnp.log2(f).astype(BF16))
        k = 1.0 - f
        operands = []
        for hh in range(HEADS_PER_GROUP):
            lanes = slice(hh * dh, (hh + 1) * dh)
            operands.append(_head_operands(reverse, b[:, lanes], k[:, lanes],
                                           None if q is None else q[:, lanes], rhs_ready))
        return operands, v, q is not None

    def first(item, prepared):
        s, g = item
        operands, v, has_q = prepared
        heads = []
        for hh in range(HEADS_PER_GROUP):
            lanes = slice(hh * dh, (hh + 1) * dh)
            if order is not None:
                order.read(s, g * HEADS_PER_GROUP + hh)
            head = _Steps(_scan_head(reverse, operands[hh], v[:, lanes], keep,
                                     st_ref[g * HEADS_PER_GROUP + hh], rhs_ready))
            for _ in range(2 if has_q else 1):
                head.step()
                yield
            heads.append(head)
        return item, heads

    def second(cont):
        (s, g), heads = cont
        for hh, head in enumerate(heads):
            h = g * HEADS_PER_GROUP + hh
            while head.step():
                yield
            o, st = head.value
            st_ref[h] = st
            if order is not None:
                order.wrote(s, h)
            if emit is not None:
                emit(s, h, o)
            yield

    items = [(s, g) for s in subtiles for g in range(HEADS // HEADS_PER_GROUP)]
    return _pipeline(items, issue, prepare, first, second, lambda item: None, lookahead)


def _ada_body(cond_ref, w_ref, b_ref, o_ref):
    o_ref[:, 0, :] = _dot(_silu(cond_ref[...]), w_ref[0]) + b_ref[0]


def _ada_mods(cond, ada_w, ada_b):
    depth, d, n = ada_w.shape
    return pl.pallas_call(
        _ada_body,
        grid=(depth, n // d),
        in_specs=[
            pl.BlockSpec((COND_ROWS, d), lambda l, j: (0, 0)),
            pl.BlockSpec((1, d, d), lambda l, j: (l, 0, j)),
            pl.BlockSpec((1, 1, d), lambda l, j: (l, 0, j)),
        ],
        out_specs=pl.BlockSpec((COND_ROWS, 1, d), lambda l, j: (l * (n // d) + j, 0, 0)),
        out_shape=jax.ShapeDtypeStruct((depth * (n // d) * COND_ROWS, 1, d), F32),
        compiler_params=_COMPILER_PARAMS,
        name="ada_mods",
    )(cond, ada_w, ada_b.reshape(depth, 1, n))


def _mod_spec(mods, layer, which, row=None):
    base = (layer * 3 + which) * COND_ROWS
    if row is None:
        return pl.BlockSpec((1, 1, mods.shape[-1]), lambda b, *_: (base + b, 0, 0))
    return pl.BlockSpec((1, 1, mods.shape[-1]), lambda *_: (base + row, 0, 0))


def _ctx_body(ctx_ref, shift_ref, scale_ref, nw_ref, w_ref, lg_ref, cum_ref, keep_ref,
              sf_ref, sb_ref, hb_ref):
    n_batch, tc, d = ctx_ref.shape
    gw = HEADS_PER_GROUP * (d // HEADS)
    subtiles = list(range(tc // MASK_ROWS))
    normalised, v = set(), {}

    def fetch(direction, bi, s, g):
        rows = _subtile_rows(s, MASK_ROWS)
        hb = hb_ref.at[bi]
        if (bi, s) not in normalised:
            hb[rows] = _normalised_rows(ctx_ref[bi, rows], shift_ref, scale_ref,
                                        nw_ref.at[0:1])
            normalised.add((bi, s))
        cols = lambda block: slice(block * d + g * gw, block * d + (g + 1) * gw)
        if (bi, s, g) not in v:
            v[bi, s, g] = _dot(hb[rows], w_ref[:, cols(2)]).astype(BF16)
            yield
        f_pre = _dot(hb[rows], w_ref[:, cols(direction)])
        yield
        return f_pre, v[bi, s, g], None

    streams = []
    for bi in range(n_batch):
        for direction, st_ref in enumerate((sf_ref, sb_ref)):
            st = st_ref.at[bi]
            st[...] = jnp.zeros(st.shape, F32)
            order = subtiles[::-1] if direction else subtiles
            streams.append(_hgrn_items(
                direction, order, lambda s, g, dr=direction, bi=bi: fetch(dr, bi, s, g), lg_ref,
                (cum_ref.at[direction], keep_ref.at[direction]), st, None))
    _interleave(*streams)


def _ctx_states(ctx, mods, ctx_row, norm_w, w_in, lb_logits, cum, keep):
    bsz, tc, d = ctx.shape
    dh = d // HEADS
    const2 = lambda b: (0, 0)
    nb = CTX_BATCH_PER_STEP
    state_spec = pl.BlockSpec((nb, HEADS, dh, dh), lambda b: (b, 0, 0, 0))
    state_shape = jax.ShapeDtypeStruct((bsz, HEADS, dh, dh), F32)
    return pl.pallas_call(
        _ctx_body,
        grid=(bsz // nb,),
        in_specs=[
            pl.BlockSpec((nb, tc, d), lambda b: (b, 0, 0)),
            _mod_spec(mods, 0, 0, ctx_row), _mod_spec(mods, 0, 1, ctx_row),
            pl.BlockSpec(norm_w.shape, const2),
            pl.BlockSpec((d, 3 * d), const2),
            pl.BlockSpec(lb_logits.shape, lambda b: (0, 0, 0)),
            pl.BlockSpec(cum.shape, lambda b: (0, 0, 0)),
            pl.BlockSpec(keep.shape, lambda b: (0, 0, 0)),
        ],
        out_specs=[state_spec, state_spec],
        out_shape=[state_shape, state_shape],
        scratch_shapes=[pltpu.VMEM((nb, tc, d), BF16)],
        compiler_params=_COMPILER_PARAMS,
        name="ctx_states",
    )(ctx, mods, mods, norm_w, w_in, lb_logits, cum, keep)


def _load_initial_state(s0_ref, st_ref):
    @pl.when(pl.program_id(1) == 0)
    def _():
        st_ref[...] = s0_ref[0]


def _fwd_body(x_ref, shift_ref, scale_ref, nw_ref, w_ref, lg_ref, cum_ref, keep_ref, s0_ref,
              of_ref, fb_ref, v_ref, q_ref, zg_ref, st_ref, hb_ref):
    d = x_ref.shape[-1]
    dh = d // HEADS
    gw = HEADS_PER_GROUP * dh
    _load_initial_state(s0_ref, st_ref)
    normalised = set()

    def fetch(s, g):
        rows = _subtile_rows(s, FWD_SUBTILE)
        if s not in normalised:
            hb_ref[rows] = _normalised_rows(x_ref[0, rows], shift_ref, scale_ref,
                                            nw_ref.at[0:1])
            normalised.add(s)
        lanes = slice(g * gw, (g + 1) * gw)
        project = lambda block: _dot(
            hb_ref[rows], w_ref[:, block * d + g * gw:block * d + (g + 1) * gw])
        f_fwd = project(0)
        yield
        v = project(2).astype(BF16)
        v_ref[0, rows, lanes] = v
        yield
        q = _silu(project(3)) * (dh ** -0.5)
        q_ref[0, rows, lanes] = q.astype(BF16)
        yield
        fb_ref[0, rows, lanes] = project(1).astype(BF16)
        yield
        zg_ref[0, rows, lanes] = _silu(project(4)).astype(BF16)
        yield
        return f_fwd, v, q

    def emit(s, h, o):
        of_ref[0, _subtile_rows(s, FWD_SUBTILE), h * dh:(h + 1) * dh] = o.astype(BF16)

    subtiles = list(range(x_ref.shape[1] // FWD_SUBTILE))
    _interleave(_hgrn_items(0, subtiles, fetch, lg_ref, (cum_ref.at[0], keep_ref.at[0]), st_ref,
                            emit, lookahead=RECURRENCE_LOOKAHEAD))


def _fwd_sweep(x, mods, norm_w, w_in, lb_logits, cum, keep, s0_f):
    bsz, t, d = x.shape
    dh = d // HEADS
    tq = FWD_TOKENS_PER_STEP
    tile = pl.BlockSpec((1, tq, d), lambda b, i: (b, i, 0))
    saved = jax.ShapeDtypeStruct((bsz, t, d), BF16)
    return pl.pallas_call(
        _fwd_body,
        grid=(bsz, t // tq),
        in_specs=[
            tile, _mod_spec(mods, 0, 0), _mod_spec(mods, 0, 1), _resident(norm_w),
            _resident(w_in), _resident(lb_logits), _resident(cum), _resident(keep),
            pl.BlockSpec((1, HEADS, dh, dh), lambda b, i: (b, 0, 0, 0)),
        ],
        out_specs=[tile] * 5,
        out_shape=[saved] * 5,
        scratch_shapes=[pltpu.VMEM((HEADS, dh, dh), F32), pltpu.VMEM((tq, d), BF16)],
        compiler_params=_COMPILER_PARAMS,
        name="hgrn_fwd_sweep",
    )(x, mods, mods, norm_w, w_in, lb_logits, cum, keep, s0_f)


def _pool_stream(x1, shift_ref, scale_ref, gate_ref, nw_ref, wi_ref, band_ref, inv_ref, wg_ref,
                 ps_ref, wo_ref, fw_ref, hb_ref, act_ref, store):
    d = x1.shape[-1]
    gd = d // len(POOL_WINDOWS)
    hb_ref[...] = _normalised_rows(x1, shift_ref, scale_ref, nw_ref)

    def issue(g):
        u = _dot(hb_ref[...], wi_ref[:, g * gd:(g + 1) * gd])
        yield
        z = _dot(hb_ref[...], wi_ref[:, d + g * gd:d + (g + 1) * gd])
        yield
        return u, z

    def prepare(g, projected):
        u, z = projected
        return u, z, u.astype(BF16)

    def first(g, prepared):
        u, z, u_bf = prepared
        band = band_ref[g]
        mean = _by_mask_rows(lambda rows: _dot(band, rows) * inv_ref[g], u_bf)
        yield
        return g, u, z, mean

    def second(cont):
        g, u, z, mean = cont
        pooled = mean - u
        y = _dot(pooled.astype(BF16), wg_ref[g]) * ps_ref[:, g * gd:(g + 1) * gd]
        act_ref[:, g * gd:(g + 1) * gd] = (y * _silu(z)).astype(BF16)
        yield

    def finish_after(g):
        if g != len(POOL_WINDOWS) - 1:
            return None

        def finish():
            mix = _dot(act_ref[...], wo_ref[...])
            store(_rmsnorm(x1 + gate_ref[0] * mix, fw_ref[...]))

        return finish

    return _pipeline(list(range(len(POOL_WINDOWS))), issue, prepare, first, second, finish_after)


def _bwd_body(x_ref, of_ref, fb_ref, v_ref, q_ref, zg_ref, gate0_ref, lg_ref, cum_ref, keep_ref,
              s0_ref, gw_ref, wo_ref, shift1_ref, scale1_ref, gate1_ref, nw_ref, wi_ref,
              band_ref, inv_ref, wg_ref, ps_ref, wo1_ref, fw_ref, out_ref, st_ref, act0_ref,
              hb1_ref, act1_ref):
    d = x_ref.shape[-1]
    dh = d // HEADS
    gw = HEADS_PER_GROUP * dh
    _load_initial_state(s0_ref, st_ref)
    masks = (cum_ref.at[1], keep_ref.at[1])

    def fetch(s, g):
        at = (0, _subtile_rows(s, BWD_SUBTILE), slice(g * gw, (g + 1) * gw))
        yield
        return fb_ref[at].astype(F32), v_ref[at], q_ref[at].astype(F32)

    def emit(s, h, o):
        at = (0, _subtile_rows(s, BWD_SUBTILE), slice(h * dh, (h + 1) * dh))
        o = _rmsnorm(of_ref[at].astype(F32) + o, gw_ref[:, at[2]])
        act0_ref[at[1:]] = (o * zg_ref[at].astype(F32)).astype(BF16)

    def all_layers(s):
        yield from _hgrn_items(1, [s], fetch, lg_ref, masks, st_ref, emit, order, BWD_RHS_READY,
                               lookahead=RECURRENCE_LOOKAHEAD)
        rows = _subtile_rows(s, BWD_SUBTILE)
        x1 = x_ref[0, rows] + gate0_ref[0] * _dot(act0_ref[rows], wo_ref[...])
        yield

        def store(out):
            out_ref[0, rows] = out

        yield from _pool_stream(x1, shift1_ref, scale1_ref, gate1_ref, nw_ref.at[1:2], wi_ref,
                                band_ref, inv_ref, wg_ref, ps_ref, wo1_ref, fw_ref,
                                hb1_ref.at[rows], act1_ref.at[rows], store)

    subtiles = list(range(x_ref.shape[1] // BWD_SUBTILE))[::-1]
    order = _StateOrder(subtiles)
    _interleave(*[all_layers(s) for s in subtiles], lag=STREAM_LAG)


def _bwd_sweep(x, saved, mods, lb_logits, cum, keep, s0_b, gnorm_w, w_out, norm_w, pool_w_in,
               band, inv_count, w_grp, pool_scale, pool_w_out, final_w):
    bsz, t, d = x.shape
    dh = d // HEADS
    tq = BWD_TOKENS_PER_STEP
    nt = t // tq
    tile = pl.BlockSpec((1, tq, d), lambda b, i: (b, nt - 1 - i, 0))
    vec = pl.BlockSpec((1, d), lambda b, i: (0, 0))
    whole = _resident
    return pl.pallas_call(
        _bwd_body,
        grid=(bsz, nt),
        in_specs=[tile] * 6 + [
            _mod_spec(mods, 0, 2), whole(lb_logits), whole(cum), whole(keep),
            pl.BlockSpec((1, HEADS, dh, dh), lambda b, i: (b, 0, 0, 0)),
            vec, whole(w_out), _mod_spec(mods, 1, 0), _mod_spec(mods, 1, 1), _mod_spec(mods, 1, 2),
            whole(norm_w), whole(pool_w_in), whole(band),
            whole(inv_count), whole(w_grp), vec, whole(pool_w_out), vec,
        ],
        out_specs=tile,
        out_shape=jax.ShapeDtypeStruct((bsz, t, d), F32),
        scratch_shapes=[pltpu.VMEM((HEADS, dh, dh), F32)] + [pltpu.VMEM((tq, d), BF16)] * 3,
        compiler_params=_COMPILER_PARAMS,
        name="hgrn_bwd_pool_sweep",
    )(x, *saved, mods, lb_logits, cum, keep, s0_b, gnorm_w, w_out, mods, mods, mods,
      norm_w, pool_w_in, band, inv_count, w_grp, pool_scale, pool_w_out, final_w)


def _cumulative_matrices(n):
    r = jnp.arange(n)[:, None]
    c = jnp.arange(n)[None, :]
    same = (r // CHUNK) == (c // CHUNK)
    cum = jnp.stack([same & (c <= r), same & (c >= r)])
    return cum.astype(BF16), cum[:, :INTRA_ROWS, :INTRA_ROWS].astype(F32)


def _pool_windows(n):
    r = jnp.arange(n)[:, None]
    c = jnp.arange(n)[None, :]
    same = (r // GRID_W) == (c // GRID_W)
    band = jnp.stack([(same & (c >= r - w // 2) & (c < r - w // 2 + w)) for w in POOL_WINDOWS])
    count = jnp.sum(band.astype(F32), axis=-1, keepdims=True)
    return band.astype(BF16), 1.0 / count


def kernel(x, c, ctx, c_ctx, ada_w, ada_b, norm_w, hgrn_w_in, hgrn_lb_logits, hgrn_gnorm_w,
           hgrn_w_out, pool_w_in, pool_w_grp, pool_scale, pool_w_out, final_norm_w):
    bsz, t, d = x.shape
    depth = ada_w.shape[0]
    assert depth == 2 and hgrn_w_in.shape[0] == 1 and pool_w_in.shape[0] == 1
    assert t % FWD_TOKENS_PER_STEP == 0 and t % BWD_TOKENS_PER_STEP == 0
    assert ctx.shape[1] % MASK_ROWS == 0 and bsz < COND_ROWS
    assert bsz % CTX_BATCH_PER_STEP == 0 and ada_w.shape[2] == 3 * d
    assert HEADS_PER_GROUP * (d // HEADS) == MXU_COLUMNS_V7X
    assert FWD_TOKENS_PER_STEP % FWD_SUBTILE == 0 and BWD_TOKENS_PER_STEP % BWD_SUBTILE == 0
    assert FWD_SUBTILE % MASK_ROWS == 0 and BWD_SUBTILE % MASK_ROWS == 0
    assert MASK_ROWS % INTRA_ROWS == 0 and INTRA_ROWS % CHUNK == 0 and MASK_ROWS % GRID_W == 0

    cond = jnp.zeros((COND_ROWS, d), F32).at[:bsz].set(c).at[bsz].set(c_ctx)
    mods = _ada_mods(cond, ada_w, ada_b)

    w_in = hgrn_w_in[0].astype(BF16)
    lb_logits = hgrn_lb_logits
    cum, keep = _cumulative_matrices(MASK_ROWS)
    band, inv_count = _pool_windows(MASK_ROWS)

    s0_f, s0_b = _ctx_states(ctx, mods, bsz, norm_w, w_in, lb_logits, cum, keep)
    saved = _fwd_sweep(x, mods, norm_w, w_in, lb_logits, cum, keep, s0_f)
    return _bwd_sweep(x, saved, mods, lb_logits, cum, keep, s0_b, hgrn_gnorm_w[0:1],
                      hgrn_w_out[0].astype(BF16), norm_w, pool_w_in[0].astype(BF16), band,
                      inv_count, pool_w_grp[0].astype(BF16), pool_scale[0:1],
                      pool_w_out[0].astype(BF16), final_norm_w.reshape(1, d))
```

```python
import jax
import jax.numpy as jnp
from jax import lax
from jax.experimental import pallas as pl
from jax.experimental.pallas import tpu as pltpu

HEADS = 8
CHUNK = 64
GRID_W = 64
POOL_WINDOWS = (2, 4, 8, 16)
EPS = 1e-6
MXU_COLUMNS_V7X = 256
VMEM_BYTES_V7X = 64 * 1024 * 1024
MASK_ROWS = 256
FWD_SUBTILE = 512
BWD_SUBTILE = 256
FWD_TOKENS_PER_STEP = 1024
BWD_TOKENS_PER_STEP = 1024
HEADS_PER_GROUP = 2
INTRA_ROWS = 128
LOOKAHEAD = 2
FWD_LOOKAHEAD = 3
FWD_RHS_READY = (True, True)
BWD_RHS_READY = (False, True)
STREAM_LAG = 20
CTX_BATCH_PER_STEP = 4
COND_ROWS = 16
VMEM_LIMIT_BYTES = VMEM_BYTES_V7X - 4 * 1024 * 1024

F32 = jnp.float32
BF16 = jnp.bfloat16

_COMPILER_PARAMS = pltpu.CompilerParams(vmem_limit_bytes=VMEM_LIMIT_BYTES)


def _silu(a):
    half = 0.5 * a
    return half + half * jnp.tanh(half)


def _rmsnorm(xf, w):
    y = xf * lax.rsqrt(jnp.mean(xf * xf, axis=-1, keepdims=True) + EPS)
    return y * w


def _normalised_rows(x, shift_ref, scale_ref, nw_ref):
    gain = nw_ref[...] * (1.0 + scale_ref[0])
    return (_rmsnorm(x, gain) + shift_ref[0]).astype(BF16)


def _dot(a, b):
    return jnp.dot(a, b, preferred_element_type=F32)


def _dot_nt(a, b):
    return lax.dot_general(a, b, (((1,), (1,)), ((), ())), preferred_element_type=F32)


def _dot_tn(a, b):
    return lax.dot_general(a, b, (((0,), (0,)), ((), ())), preferred_element_type=F32)


def _lower_bound(lg_ref, direction, layer):
    rows = [lg_ref[n, direction:direction + 1, :] for n in range(lg_ref.shape[0])]
    m = jnp.zeros_like(rows[0])
    for r in rows:
        m = jnp.maximum(m, r)
    exps = [jnp.exp(r - m) for r in rows]
    denom = jnp.exp(-m)
    for e in exps:
        denom = denom + e
    acc = exps[0]
    for e in exps[1:layer + 1]:
        acc = acc + e
    return acc / denom


def _subtile_rows(s, size):
    return slice(s * size, (s + 1) * size)


def _by_mask_rows(fn, *arrays):
    n = arrays[0].shape[0] // MASK_ROWS
    blocks = [fn(*(a[i * MASK_ROWS:(i + 1) * MASK_ROWS] for a in arrays)) for i in range(n)]
    return blocks[0] if n == 1 else jnp.concatenate(blocks, axis=0)


def _resident(a):
    return pl.BlockSpec(a.shape, lambda b, i: (0,) * a.ndim, pipeline_mode=pl.Buffered(1))


class _Steps:
    def __init__(self, gen):
        self._gen = gen
        self.done = False
        self.value = None

    def step(self):
        if not self.done:
            try:
                next(self._gen)
            except StopIteration as stop:
                self.done = True
                self.value = stop.value
        return not self.done


def _alongside(main, side):
    while main.step():
        yield
        if side.step():
            yield
    return main.value


def _pipeline(items, issue, prepare, first, second, finish_after, lookahead=LOOKAHEAD):
    ahead = min(lookahead, len(items))
    issued = []
    for item in items[:ahead]:
        issued.append((yield from issue(item)))
    prepared = []
    for item, handle in zip(items, issued):
        prepared.append(prepare(item, handle))
        yield
    deferred = []
    for i, item in enumerate(items):
        later = items[i + ahead] if i + ahead < len(items) else None
        issuing = _Steps(issue(later) if later is not None else iter(()))
        cont = yield from _alongside(_Steps(first(item, prepared[i])), issuing)
        for fn in deferred:
            fn()
            yield
        deferred = []
        yield from _alongside(_Steps(second(cont)), issuing)
        while issuing.step():
            yield
        fn = finish_after(item)
        if fn is not None:
            deferred.append(fn)
        if later is not None:
            prepared.append(prepare(later, issuing.value))
            yield
    for fn in deferred:
        fn()
        yield


def _interleave(*streams, lag=0):
    done = [False] * len(streams)
    rnd = 0
    while not all(done):
        for k, stream in enumerate(streams):
            if done[k] or rnd < k * lag:
                continue
            try:
                next(stream)
            except StopIteration:
                done[k] = True
        rnd += 1


def _head_operands(reverse, b, k, q, rhs_ready):
    nc = b.shape[0] // CHUNK
    mid = CHUNK // 2 if reverse else CHUNK // 2 - 1
    last = 0 if reverse else CHUNK - 1
    if q is None:
        b_last = [b[n * CHUNK + last:n * CHUNK + last + 1] for n in range(nc)]
        to_end, total = {}, jnp.zeros_like(b_last[0])
        for n in (range(nc) if reverse else range(nc - 1, -1, -1)):
            to_end[n] = total
            total = total + b_last[n]
        k_end = [k[n * CHUNK:(n + 1) * CHUNK]
                 * jnp.exp2(b_last[n] + to_end[n] - b[n * CHUNK:(n + 1) * CHUNK])
                 for n in range(nc)]
        return [], [], [jnp.concatenate(k_end, axis=0).astype(BF16)], [], [jnp.exp2(total)]
    k_dec, q_dec, k_end, q_in, decay = [], [], [], [], []
    for n in range(nc):
        rows = slice(n * CHUNK, (n + 1) * CHUNK)
        bn = b[rows]
        b_mid = bn[mid:mid + 1]
        b_last = bn[last:last + 1]
        kd = k[rows] * jnp.exp2(b_mid - bn)
        k_dec.append(kd if rhs_ready[0] else kd.astype(BF16))
        k_end.append((kd * jnp.exp2(b_last - b_mid)).astype(BF16))
        decay.append(jnp.exp2(b_last))
        qd = q[rows] * jnp.exp2(bn - b_mid)
        q_dec.append(qd.astype(BF16))
        q_in.append((qd * jnp.exp2(b_mid)).astype(BF16))
    if rhs_ready[0]:
        per_block = INTRA_ROWS // CHUNK
        k_dec = [jnp.concatenate(k_dec[j:j + per_block], axis=0).T.astype(BF16)
                 for j in range(0, nc, per_block)]
    return k_dec, q_dec, k_end, q_in, decay


def _scan_head(reverse, operands, v, keep, st, rhs_ready):
    k_dec, q_dec, k_end, q_in, decay = operands
    nc = len(decay)
    per_block = INTRA_ROWS // CHUNK
    scores = []
    if q_dec:
        for j in range(nc // per_block):
            blk = slice(j * per_block, (j + 1) * per_block)
            q_blk = jnp.concatenate(q_dec[blk], axis=0)
            if rhs_ready[0]:
                scores.append(_dot(q_blk, k_dec[j]))
            else:
                scores.append(_dot_nt(q_blk, jnp.concatenate(k_dec[blk], axis=0)))
        yield
    span = v.shape[0] // nc
    update = [_dot_tn(v[n * span:(n + 1) * span], k_end[n]) for n in range(nc)]
    yield
    if q_dec:
        scores = [jnp.where(keep, a, 0.0).astype(BF16) for a in scores]
        o_intra = [_dot(a, v[j * INTRA_ROWS:(j + 1) * INTRA_ROWS]) for j, a in enumerate(scores)]
        yield
    carried = [None] * nc
    for n in (range(nc - 1, -1, -1) if reverse else range(nc)):
        carried[n] = st.T.astype(BF16) if rhs_ready[1] else st.astype(BF16)
        st = st * decay[n] + update[n]
    if not q_dec:
        return None, st
    o_inter = [(_dot if rhs_ready[1] else _dot_nt)(q_in[n], carried[n]) for n in range(nc)]
    return jnp.concatenate(o_intra, axis=0) + jnp.concatenate(o_inter, axis=0), st


class _StateOrder:
    def __init__(self, subtiles):
        self._before = {s: subtiles[:i] for i, s in enumerate(subtiles)}
        self._written = set()

    def read(self, s, h):
        missing = [p for p in self._before[s] if (p, h) not in self._written]
        assert not missing, f"state of head {h} read for subtile {s} before {missing} wrote it"

    def wrote(self, s, h):
        self._written.add((s, h))


def _hgrn_items(direction, subtiles, fetch, lg_ref, masks, st_ref, emit, order=None,
                rhs_ready=FWD_RHS_READY, lookahead=LOOKAHEAD):
    dh = st_ref.shape[-1]
    gw = HEADS_PER_GROUP * dh
    reverse = direction == 1
    cum_ref, keep_ref = masks
    cum = cum_ref[...]
    keep = keep_ref[...] != 0.0
    lb_all = _lower_bound(lg_ref, direction, 0)

    def issue(item):
        return (yield from fetch(*item))

    def prepare(item, fetched):
        s, g = item
        f_pre, v, q = fetched
        lb = lb_all[:, g * gw:(g + 1) * gw]
        f = 0.5 * (1.0 + lb) + (0.5 * (1.0 - lb)) * jnp.tanh(0.5 * f_pre)
        b = _by_mask_rows(lambda logf: _dot(cum, logf), jnp.log2(f).astype(BF16))
        k = 1.0 - f
        operands = []
        for hh in range(HEADS_PER_GROUP):
            lanes = slice(hh * dh, (hh + 1) * dh)
            operands.append(_head_operands(reverse, b[:, lanes], k[:, lanes],
                                           None if q is None else q[:, lanes], rhs_ready))
        return operands, v, q is not None

    def first(item, prepared):
        s, g = item
        operands, v, has_q = prepared
        heads = []
        for hh in range(HEADS_PER_GROUP):
            lanes = slice(hh * dh, (hh + 1) * dh)
            if order is not None:
                order.read(s, g * HEADS_PER_GROUP + hh)
            head = _Steps(_scan_head(reverse, operands[hh], v[:, lanes], keep,
                                     st_ref[g * HEADS_PER_GROUP + hh], rhs_ready))
            for _ in range(2 if has_q else 1):
                head.step()
                yield
            heads.append(head)
        return item, heads

    def second(cont):
        (s, g), heads = cont
        for hh, head in enumerate(heads):
            h = g * HEADS_PER_GROUP + hh
            while head.step():
                yield
            o, st = head.value
            st_ref[h] = st
            if order is not None:
                order.wrote(s, h)
            if emit is not None:
                emit(s, h, o)
            yield

    items = [(s, g) for s in subtiles for g in range(HEADS // HEADS_PER_GROUP)]
    return _pipeline(items, issue, prepare, first, second, lambda item: None, lookahead)


def _ada_body(cond_ref, w_ref, b_ref, o_ref):
    o_ref[:, 0, :] = _dot(_silu(cond_ref[...]).astype(BF16), w_ref[0].astype(BF16)) + b_ref[0]


def _ada_mods(cond, ada_w, ada_b):
    depth, d, n = ada_w.shape
    return pl.pallas_call(
        _ada_body,
        grid=(depth, n // d),
        in_specs=[
            pl.BlockSpec((COND_ROWS, d), lambda l, j: (0, 0)),
            pl.BlockSpec((1, d, d), lambda l, j: (l, 0, j)),
            pl.BlockSpec((1, 1, d), lambda l, j: (l, 0, j)),
        ],
        out_specs=pl.BlockSpec((COND_ROWS, 1, d), lambda l, j: (l * (n // d) + j, 0, 0)),
        out_shape=jax.ShapeDtypeStruct((depth * (n // d) * COND_ROWS, 1, d), F32),
        compiler_params=_COMPILER_PARAMS,
        name="ada_mods",
    )(cond, ada_w, ada_b.reshape(depth, 1, n))


def _mod_spec(mods, layer, which, row=None):
    base = (layer * 3 + which) * COND_ROWS
    if row is None:
        return pl.BlockSpec((1, 1, mods.shape[-1]), lambda b, *_: (base + b, 0, 0))
    return pl.BlockSpec((1, 1, mods.shape[-1]), lambda *_: (base + row, 0, 0))


def _ctx_body(ctx_ref, shift_ref, scale_ref, nw_ref, w_ref, lg_ref, cum_ref, keep_ref,
              sf_ref, sb_ref, hb_ref):
    n_batch, tc, d = ctx_ref.shape
    gw = HEADS_PER_GROUP * (d // HEADS)
    subtiles = list(range(tc // MASK_ROWS))
    normalised, v = set(), {}

    def fetch(direction, bi, s, g):
        rows = _subtile_rows(s, MASK_ROWS)
        hb = hb_ref.at[bi]
        if (bi, s) not in normalised:
            hb[rows] = _normalised_rows(ctx_ref[bi, rows], shift_ref, scale_ref,
                                        nw_ref.at[0:1])
            normalised.add((bi, s))
        cols = lambda block: slice(block * d + g * gw, block * d + (g + 1) * gw)
        if (bi, s, g) not in v:
            v[bi, s, g] = _dot(hb[rows], w_ref[:, cols(2)]).astype(BF16)
            yield
        f_pre = _dot(hb[rows], w_ref[:, cols(direction)])
        yield
        return f_pre, v[bi, s, g], None

    streams = []
    for bi in range(n_batch):
        for direction, st_ref in enumerate((sf_ref, sb_ref)):
            st = st_ref.at[bi]
            st[...] = jnp.zeros(st.shape, F32)
            order = subtiles[::-1] if direction else subtiles
            streams.append(_hgrn_items(
                direction, order, lambda s, g, dr=direction, bi=bi: fetch(dr, bi, s, g), lg_ref,
                (cum_ref.at[direction], keep_ref.at[direction]), st, None))
    _interleave(*streams)


def _ctx_states(ctx, mods, ctx_row, norm_w, w_in, lb_logits, cum, keep):
    bsz, tc, d = ctx.shape
    dh = d // HEADS
    const2 = lambda b: (0, 0)
    nb = CTX_BATCH_PER_STEP
    state_spec = pl.BlockSpec((nb, HEADS, dh, dh), lambda b: (b, 0, 0, 0))
    state_shape = jax.ShapeDtypeStruct((bsz, HEADS, dh, dh), F32)
    return pl.pallas_call(
        _ctx_body,
        grid=(bsz // nb,),
        in_specs=[
            pl.BlockSpec((nb, tc, d), lambda b: (b, 0, 0)),
            _mod_spec(mods, 0, 0, ctx_row), _mod_spec(mods, 0, 1, ctx_row),
            pl.BlockSpec(norm_w.shape, const2),
            pl.BlockSpec((d, 3 * d), const2),
            pl.BlockSpec(lb_logits.shape, lambda b: (0, 0, 0)),
            pl.BlockSpec(cum.shape, lambda b: (0, 0, 0)),
            pl.BlockSpec(keep.shape, lambda b: (0, 0, 0)),
        ],
        out_specs=[state_spec, state_spec],
        out_shape=[state_shape, state_shape],
        scratch_shapes=[pltpu.VMEM((nb, tc, d), BF16)],
        compiler_params=_COMPILER_PARAMS,
        name="ctx_states",
    )(ctx, mods, mods, norm_w, w_in, lb_logits, cum, keep)


def _load_initial_state(s0_ref, st_ref):
    @pl.when(pl.program_id(1) == 0)
    def _():
        st_ref[...] = s0_ref[0]


def _fwd_body(x_ref, shift_ref, scale_ref, nw_ref, w_ref, lg_ref, cum_ref, keep_ref, s0_ref,
              of_ref, fb_ref, v_ref, q_ref, zg_ref, st_ref, hb_ref):
    d = x_ref.shape[-1]
    dh = d // HEADS
    gw = HEADS_PER_GROUP * dh
    _load_initial_state(s0_ref, st_ref)
    normalised = set()

    def fetch(s, g):
        rows = _subtile_rows(s, FWD_SUBTILE)
        if s not in normalised:
            hb_ref[rows] = _normalised_rows(x_ref[0, rows], shift_ref, scale_ref,
                                            nw_ref.at[0:1])
            normalised.add(s)
        lanes = slice(g * gw, (g + 1) * gw)
        project = lambda block: _dot(
            hb_ref[rows], w_ref[:, block * d + g * gw:block * d + (g + 1) * gw])
        f_fwd = project(0)
        yield
        v = project(2).astype(BF16)
        v_ref[0, rows, lanes] = v
        yield
        q = _silu(project(3)) * (dh ** -0.5)
        q_ref[0, rows, lanes] = q.astype(BF16)
        yield
        fb_ref[0, rows, lanes] = project(1).astype(BF16)
        yield
        zg_ref[0, rows, lanes] = _silu(project(4)).astype(BF16)
        yield
        return f_fwd, v, q

    def emit(s, h, o):
        of_ref[0, _subtile_rows(s, FWD_SUBTILE), h * dh:(h + 1) * dh] = o.astype(BF16)

    subtiles = list(range(x_ref.shape[1] // FWD_SUBTILE))
    _interleave(_hgrn_items(0, subtiles, fetch, lg_ref, (cum_ref.at[0], keep_ref.at[0]), st_ref,
                            emit, lookahead=FWD_LOOKAHEAD))


def _fwd_sweep(x, mods, norm_w, w_in, lb_logits, cum, keep, s0_f):
    bsz, t, d = x.shape
    dh = d // HEADS
    tq = FWD_TOKENS_PER_STEP
    tile = pl.BlockSpec((1, tq, d), lambda b, i: (b, i, 0))
    saved = jax.ShapeDtypeStruct((bsz, t, d), BF16)
    return pl.pallas_call(
        _fwd_body,
        grid=(bsz, t // tq),
        in_specs=[
            tile, _mod_spec(mods, 0, 0), _mod_spec(mods, 0, 1), _resident(norm_w),
            _resident(w_in), _resident(lb_logits), _resident(cum), _resident(keep),
            pl.BlockSpec((1, HEADS, dh, dh), lambda b, i: (b, 0, 0, 0)),
        ],
        out_specs=[tile] * 5,
        out_shape=[saved] * 5,
        scratch_shapes=[pltpu.VMEM((HEADS, dh, dh), F32), pltpu.VMEM((tq, d), BF16)],
        compiler_params=_COMPILER_PARAMS,
        name="hgrn_fwd_sweep",
    )(x, mods, mods, norm_w, w_in, lb_logits, cum, keep, s0_f)


def _pool_stream(x1, shift_ref, scale_ref, gate_ref, nw_ref, wi_ref, band_ref, inv_ref, wg_ref,
                 ps_ref, wo_ref, fw_ref, hb_ref, act_ref, store):
    d = x1.shape[-1]
    gd = d // len(POOL_WINDOWS)
    hb_ref[...] = _normalised_rows(x1, shift_ref, scale_ref, nw_ref)

    def issue(g):
        u = _dot(hb_ref[...], wi_ref[:, g * gd:(g + 1) * gd])
        yield
        z = _dot(hb_ref[...], wi_ref[:, d + g * gd:d + (g + 1) * gd])
        yield
        return u, z

    def prepare(g, projected):
        u, z = projected
        return u, z, u.astype(BF16)

    def first(g, prepared):
        u, z, u_bf = prepared
        band = band_ref[g]
        mean = _by_mask_rows(lambda rows: _dot(band, rows) * inv_ref[g], u_bf)
        yield
        return g, u, z, mean

    def second(cont):
        g, u, z, mean = cont
        pooled = mean - u
        y = _dot(pooled.astype(BF16), wg_ref[g]) * ps_ref[:, g * gd:(g + 1) * gd]
        act_ref[:, g * gd:(g + 1) * gd] = (y * _silu(z)).astype(BF16)
        yield

    def finish_after(g):
        if g != len(POOL_WINDOWS) - 1:
            return None

        def finish():
            mix = _dot(act_ref[...], wo_ref[...])
            store(_rmsnorm(x1 + gate_ref[0] * mix, fw_ref[...]))

        return finish

    return _pipeline(list(range(len(POOL_WINDOWS))), issue, prepare, first, second, finish_after)


def _bwd_body(x_ref, of_ref, fb_ref, v_ref, q_ref, zg_ref, gate0_ref, lg_ref, cum_ref, keep_ref,
              s0_ref, gw_ref, wo_ref, shift1_ref, scale1_ref, gate1_ref, nw_ref, wi_ref,
              band_ref, inv_ref, wg_ref, ps_ref, wo1_ref, fw_ref, out_ref, st_ref, act0_ref,
              hb1_ref, act1_ref):
    d = x_ref.shape[-1]
    dh = d // HEADS
    gw = HEADS_PER_GROUP * dh
    _load_initial_state(s0_ref, st_ref)
    masks = (cum_ref.at[1], keep_ref.at[1])

    def fetch(s, g):
        at = (0, _subtile_rows(s, BWD_SUBTILE), slice(g * gw, (g + 1) * gw))
        yield
        return fb_ref[at].astype(F32), v_ref[at], q_ref[at].astype(F32)

    def emit(s, h, o):
        at = (0, _subtile_rows(s, BWD_SUBTILE), slice(h * dh, (h + 1) * dh))
        o = _rmsnorm(of_ref[at].astype(F32) + o, gw_ref[:, at[2]])
        act0_ref[at[1:]] = (o * zg_ref[at].astype(F32)).astype(BF16)

    def all_layers(s):
        yield from _hgrn_items(1, [s], fetch, lg_ref, masks, st_ref, emit, order, BWD_RHS_READY)
        rows = _subtile_rows(s, BWD_SUBTILE)
        x1 = x_ref[0, rows] + gate0_ref[0] * _dot(act0_ref[rows], wo_ref[...])
        yield

        def store(out):
            out_ref[0, rows] = out

        yield from _pool_stream(x1, shift1_ref, scale1_ref, gate1_ref, nw_ref.at[1:2], wi_ref,
                                band_ref, inv_ref, wg_ref, ps_ref, wo1_ref, fw_ref,
                                hb1_ref.at[rows], act1_ref.at[rows], store)

    subtiles = list(range(x_ref.shape[1] // BWD_SUBTILE))[::-1]
    order = _StateOrder(subtiles)
    _interleave(*[all_layers(s) for s in subtiles], lag=STREAM_LAG)


def _bwd_sweep(x, saved, mods, lb_logits, cum, keep, s0_b, gnorm_w, w_out, norm_w, pool_w_in,
               band, inv_count, w_grp, pool_scale, pool_w_out, final_w):
    bsz, t, d = x.shape
    dh = d // HEADS
    tq = BWD_TOKENS_PER_STEP
    nt = t // tq
    tile = pl.BlockSpec((1, tq, d), lambda b, i: (b, nt - 1 - i, 0))
    vec = pl.BlockSpec((1, d), lambda b, i: (0, 0))
    whole = _resident
    return pl.pallas_call(
        _bwd_body,
        grid=(bsz, nt),
        in_specs=[tile] * 6 + [
            _mod_spec(mods, 0, 2), whole(lb_logits), whole(cum), whole(keep),
            pl.BlockSpec((1, HEADS, dh, dh), lambda b, i: (b, 0, 0, 0)),
            vec, whole(w_out), _mod_spec(mods, 1, 0), _mod_spec(mods, 1, 1), _mod_spec(mods, 1, 2),
            whole(norm_w), whole(pool_w_in), whole(band),
            whole(inv_count), whole(w_grp), vec, whole(pool_w_out), vec,
        ],
        out_specs=tile,
        out_shape=jax.ShapeDtypeStruct((bsz, t, d), F32),
        scratch_shapes=[pltpu.VMEM((HEADS, dh, dh), F32)] + [pltpu.VMEM((tq, d), BF16)] * 3,
        compiler_params=_COMPILER_PARAMS,
        name="hgrn_bwd_pool_sweep",
    )(x, *saved, mods, lb_logits, cum, keep, s0_b, gnorm_w, w_out, mods, mods, mods,
      norm_w, pool_w_in, band, inv_count, w_grp, pool_scale, pool_w_out, final_w)


def _cumulative_matrices(n):
    r = jnp.arange(n)[:, None]
    c = jnp.arange(n)[None, :]
    same = (r // CHUNK) == (c // CHUNK)
    cum = jnp.stack([same & (c <= r), same & (c >= r)])
    return cum.astype(BF16), cum[:, :INTRA_ROWS, :INTRA_ROWS].astype(F32)


def _pool_windows(n):
    r = jnp.arange(n)[:, None]
    c = jnp.arange(n)[None, :]
    same = (r // GRID_W) == (c // GRID_W)
    band = jnp.stack([(same & (c >= r - w // 2) & (c < r - w // 2 + w)) for w in POOL_WINDOWS])
    count = jnp.sum(band.astype(F32), axis=-1, keepdims=True)
    return band.astype(BF16), 1.0 / count


def kernel(x, c, ctx, c_ctx, ada_w, ada_b, norm_w, hgrn_w_in, hgrn_lb_logits, hgrn_gnorm_w,
           hgrn_w_out, pool_w_in, pool_w_grp, pool_scale, pool_w_out, final_norm_w):
    bsz, t, d = x.shape
    depth = ada_w.shape[0]
    assert depth == 2 and hgrn_w_in.shape[0] == 1 and pool_w_in.shape[0] == 1
    assert t % FWD_TOKENS_PER_STEP == 0 and t % BWD_TOKENS_PER_STEP == 0
    assert ctx.shape[1] % MASK_ROWS == 0 and bsz < COND_ROWS
    assert bsz % CTX_BATCH_PER_STEP == 0 and ada_w.shape[2] == 3 * d
    assert HEADS_PER_GROUP * (d // HEADS) == MXU_COLUMNS_V7X
    assert FWD_TOKENS_PER_STEP % FWD_SUBTILE == 0 and BWD_TOKENS_PER_STEP % BWD_SUBTILE == 0
    assert FWD_SUBTILE % MASK_ROWS == 0 and BWD_SUBTILE % MASK_ROWS == 0
    assert MASK_ROWS % INTRA_ROWS == 0 and INTRA_ROWS % CHUNK == 0 and MASK_ROWS % GRID_W == 0

    cond = jnp.zeros((COND_ROWS, d), F32).at[:bsz].set(c).at[bsz].set(c_ctx)
    mods = _ada_mods(cond, ada_w, ada_b)

    w_in = hgrn_w_in[0].astype(BF16)
    lb_logits = hgrn_lb_logits
    cum, keep = _cumulative_matrices(MASK_ROWS)
    band, inv_count = _pool_windows(MASK_ROWS)

    s0_f, s0_b = _ctx_states(ctx, mods, bsz, norm_w, w_in, lb_logits, cum, keep)
    saved = _fwd_sweep(x, mods, norm_w, w_in, lb_logits, cum, keep, s0_f)
    return _bwd_sweep(x, saved, mods, lb_logits, cum, keep, s0_b, hgrn_gnorm_w[0:1],
                      hgrn_w_out[0].astype(BF16), norm_w, pool_w_in[0].astype(BF16), band,
                      inv_count, pool_w_grp[0].astype(BF16), pool_scale[0:1],
                      pool_w_out[0].astype(BF16), final_norm_w.reshape(1, d))
```
